```python
import jax, jax.numpy as jnp
from jax import lax
import numpy as np

D_MODEL = 1024
BATCH = 4
SEQ = 4096
DEPTH = 4
DEC_BATCH = 128
DEC_SEQ = 1
PAST_LEN = 2048
PAGE_SIZE = 128

N_META = 16
GDN_HEADS = 4
GDN_DK = 128
GDN_DV = 128
GDN_WIDTH = GDN_HEADS * GDN_DV
CONV_WIDTH = 4
CONV_DIM = 2 * GDN_HEADS * GDN_DK + GDN_WIDTH
GDN_CHUNK = 64
SB_HEADS = 8
SB_DH = 64
SB_WIDTH = SB_HEADS * SB_DH
SB_BLOCK = 128
MIX_WIDTH = GDN_WIDTH + SB_WIDTH
IN_COLS = CONV_DIM + GDN_WIDTH + 2 * GDN_HEADS + 3 * SB_WIDTH
D_FF = 4 * D_MODEL
EPS = 1e-6

kernel_name = 'hymba_gdn_stickbreaking_decode_step'


def rmsnorm(x, w):
    xf = x.astype(jnp.float32)
    y = xf * lax.rsqrt(jnp.mean(xf * xf, axis=-1, keepdims=True) + EPS)
    return (y * w.astype(jnp.float32)).astype(x.dtype)


def l2norm(x):
    return x * lax.rsqrt(jnp.sum(x * x, axis=-1, keepdims=True) + EPS)


def split_cols(p):
    c0 = CONV_DIM
    c1 = c0 + GDN_WIDTH
    c2 = c1 + GDN_HEADS
    c3 = c2 + GDN_HEADS
    c4 = c3 + SB_WIDTH
    c5 = c4 + SB_WIDTH
    return (p[..., :c0], p[..., c0:c1], p[..., c1:c2], p[..., c2:c3],
            p[..., c3:c4], p[..., c4:c5], p[..., c5:])


def causal_conv(xp, w, t):
    out = xp[:, 0:t] * w[0]
    for i in range(1, CONV_WIDTH):
        out = out + xp[:, i:i + t] * w[i]
    return out


def gdn_inputs(conv_out, a, b, a_log, dt_bias):
    bsz, t = conv_out.shape[:2]
    u = jax.nn.silu(conv_out.astype(jnp.float32))
    hk = GDN_HEADS * GDN_DK
    q = l2norm(u[..., :hk].reshape(bsz, t, GDN_HEADS, GDN_DK)) * (GDN_DK ** -0.5)
    k = l2norm(u[..., hk:2 * hk].reshape(bsz, t, GDN_HEADS, GDN_DK))
    v = u[..., 2 * hk:].reshape(bsz, t, GDN_HEADS, GDN_DV)
    g = -jnp.exp(a_log.astype(jnp.float32)) * jax.nn.softplus(
        a.astype(jnp.float32) + dt_bias.astype(jnp.float32))
    beta = jax.nn.sigmoid(b.astype(jnp.float32))
    return q, k, v, g, beta


def gdn_chunked(q, k, v, g, beta, s0, c):
    bsz, t, nh, _ = q.shape
    n = t // c

    def blocks(x):
        x = x.reshape((bsz, n, c, nh) + x.shape[3:])
        return jnp.moveaxis(x, (1, 3), (0, 2))

    qc, kc, vc, gc, bc = blocks(q), blocks(k), blocks(v), blocks(g), blocks(beta)
    cum = jnp.cumsum(gc, axis=-1)
    idx = jnp.arange(c)
    causal = idx[:, None] >= idx[None, :]
    strict = idx[:, None] > idx[None, :]
    diff = cum[..., :, None] - cum[..., None, :]
    gam = jnp.where(causal, jnp.exp(jnp.where(causal, diff, 0.0)), 0.0)
    kb = kc * bc[..., None]
    a_mat = jnp.where(strict, jnp.einsum('nbhid,nbhjd->nbhij', kb, kc) * gam, 0.0) + jnp.eye(c, dtype=jnp.float32)
    u = lax.linalg.triangular_solve(a_mat, vc * bc[..., None], left_side=True, lower=True, unit_diagonal=True)
    w = lax.linalg.triangular_solve(a_mat, kb * jnp.exp(cum)[..., None], left_side=True, lower=True, unit_diagonal=True)
    qk = jnp.einsum('nbhid,nbhjd->nbhij', qc, kc) * gam

    def step(s, inp):
        q_n, k_n, u_n, w_n, qk_n, cum_n = inp
        v_new = u_n - jnp.einsum('bhcd,bhde->bhce', w_n, s)
        o_n = (jnp.einsum('bhcd,bhde->bhce', q_n * jnp.exp(cum_n)[..., None], s)
               + jnp.einsum('bhij,bhje->bhie', qk_n, v_new))
        g_last = cum_n[..., -1:]
        s = (s * jnp.exp(g_last)[..., None]
             + jnp.einsum('bhcd,bhce->bhde', k_n * jnp.exp(g_last - cum_n)[..., None], v_new))
        return s, o_n

    s, o = lax.scan(step, s0, (qc, kc, u, w, qk, cum))
    o = jnp.moveaxis(o, (0, 2), (1, 3)).reshape(bsz, t, nh, GDN_DV)
    return o, s


def gdn_recurrent(q, k, v, g, beta, s0):
    def step(s, inp):
        q_t, k_t, v_t, g_t, b_t = inp
        s = s * jnp.exp(g_t)[..., None, None]
        err = v_t - jnp.einsum('bhd,bhde->bhe', k_t, s)
        s = s + jnp.einsum('bhd,bhe->bhde', k_t, err * b_t[..., None])
        return s, jnp.einsum('bhd,bhde->bhe', q_t, s)

    xs = (jnp.moveaxis(q, 1, 0), jnp.moveaxis(k, 1, 0), jnp.moveaxis(v, 1, 0),
          jnp.moveaxis(g, 1, 0), jnp.moveaxis(beta, 1, 0))
    s, o = lax.scan(step, s0, xs)
    return jnp.moveaxis(o, 0, 1), s


def gdn_output(o, z, w, dtype):
    bsz, t = o.shape[:2]
    zf = z.astype(jnp.float32).reshape(bsz, t, GDN_HEADS, GDN_DV)
    y = o * lax.rsqrt(jnp.mean(o * o, axis=-1, keepdims=True) + EPS) * w.astype(jnp.float32) * jax.nn.silu(zf)
    return y.reshape(bsz, t, GDN_WIDTH).astype(dtype)


def sb_heads(sq, sk, sv, q_norm_w, k_norm_w):
    bsz, t = sq.shape[:2]
    sh = (bsz, t, SB_HEADS, SB_DH)
    return rmsnorm(sq.reshape(sh), q_norm_w), rmsnorm(sk.reshape(sh), k_norm_w), sv.reshape(sh)


def sb_attend(q, k, v, q_pos, k_pos, bias):
    z = (jnp.einsum('bqhd,bshd->bhqs', q.astype(jnp.float32), k.astype(jnp.float32)) * (SB_DH ** -0.5)
         + bias.astype(jnp.float32)[None, :, None, None])
    visible = k_pos[None, :] < q_pos[:, None]
    log_stay = jnp.where(visible, jax.nn.log_sigmoid(-z), 0.0)
    between = lax.cumsum(log_stay, axis=3, reverse=True) - log_stay
    weights = jnp.where(visible, jnp.exp(jax.nn.log_sigmoid(z) + between), 0.0)
    return jnp.einsum('bhqs,bshd->bqhd', weights, v.astype(jnp.float32))


def sb_prompt(q, k, v, bias):
    bsz, t = q.shape[:2]
    tp = -(-t // SB_BLOCK) * SB_BLOCK
    nb = tp // SB_BLOCK
    pad = ((0, 0), (0, tp - t), (0, 0), (0, 0))
    q, k, v = jnp.pad(q, pad), jnp.pad(k, pad), jnp.pad(v, pad)
    pos = jnp.arange(tp, dtype=jnp.int32)
    q_blocks = jnp.moveaxis(q.reshape(bsz, nb, SB_BLOCK, SB_HEADS, SB_DH), 1, 0)
    o = lax.map(lambda blk: sb_attend(blk[0], k, v, blk[1], pos, bias), (q_blocks, pos.reshape(nb, SB_BLOCK)))
    o = jnp.moveaxis(o, 0, 1).reshape(bsz, tp, SB_WIDTH)
    return o[:, :t]


def sq_relu_mlp(x, w_up, w_down):
    return jnp.square(jax.nn.relu(x @ w_up)) @ w_down


def setup_inputs(seed: int = 0) -> dict:
    key = jax.random.key(seed)
    ks = jax.random.split(key, 22)
    f32 = jnp.float32
    n_pages = PAST_LEN // PAGE_SIZE
    n_used = DEC_BATCH * n_pages
    n_pool = n_used + (n_used + 3) // 4
    page_table = jax.random.permutation(ks[0], n_pool)[:n_used].reshape(DEC_BATCH, n_pages).astype(jnp.int32)

    def nrm(k, shape, scale):
        return jax.random.normal(k, shape, f32) * scale

    sb_bias = (-jnp.linspace(3.0, 8.0, SB_HEADS, dtype=f32))[None, :] + nrm(ks[20], (DEPTH, SB_HEADS), 0.1)
    return {
        'x_prompt': nrm(ks[1], (BATCH, SEQ, D_MODEL), 1.0),
        'x_sample': nrm(ks[2], (DEC_BATCH, DEC_SEQ, D_MODEL), 1.0),
        'cache_k': nrm(ks[3], (DEPTH, n_pool, PAGE_SIZE, SB_HEADS, SB_DH), 1.0),
        'cache_v': nrm(ks[4], (DEPTH, n_pool, PAGE_SIZE, SB_HEADS, SB_DH), 1.0),
        'page_table': page_table,
        'state_gdn': nrm(ks[5], (DEPTH, DEC_BATCH, GDN_HEADS, GDN_DK, GDN_DV), 0.5),
        'state_conv': nrm(ks[6], (DEPTH, DEC_BATCH, CONV_WIDTH - 1, CONV_DIM), 1.0),
        'meta_tokens': nrm(ks[7], (N_META, D_MODEL), 1.0),
        'norm1_w': 1.0 + nrm(ks[8], (DEPTH, D_MODEL), 0.01),
        'w_in': nrm(ks[9], (DEPTH, D_MODEL, IN_COLS), D_MODEL ** -0.5),
        'conv_w': nrm(ks[10], (DEPTH, CONV_WIDTH, CONV_DIM), CONV_WIDTH ** -0.5),
        'a_log': jnp.log(jax.random.uniform(ks[11], (DEPTH, GDN_HEADS), f32, 0.5, 4.0)),
        'dt_bias': nrm(ks[12], (DEPTH, GDN_HEADS), 0.1),
        'gdn_norm_w': 1.0 + nrm(ks[13], (DEPTH, GDN_DV), 0.01),
        'q_norm_w': 1.0 + nrm(ks[14], (DEPTH, SB_DH), 0.01),
        'k_norm_w': 1.0 + nrm(ks[15], (DEPTH, SB_DH), 0.01),
        'sb_bias': sb_bias,
        'w_o': nrm(ks[16], (DEPTH, MIX_WIDTH, D_MODEL), MIX_WIDTH ** -0.5),
        'norm2_w': 1.0 + nrm(ks[17], (DEPTH, D_MODEL), 0.01),
        'w_up': nrm(ks[18], (DEPTH, D_MODEL, D_FF), D_MODEL ** -0.5),
        'w_down': nrm(ks[19], (DEPTH, D_FF, D_MODEL), D_FF ** -0.5),
    }


def reference(x_prompt, x_sample, cache_k, cache_v, page_table, state_gdn, state_conv,
              meta_tokens, norm1_w, w_in, conv_w, a_log, dt_bias, gdn_norm_w,
              q_norm_w, k_norm_w, sb_bias, w_o, norm2_w, w_up, w_down):
    dt = x_prompt.dtype
    meta = jnp.broadcast_to(meta_tokens.astype(dt)[None], (x_prompt.shape[0], N_META, D_MODEL))
    xp = jnp.concatenate([meta, x_prompt], axis=1)
    xs = x_sample
    bp, tpr = xp.shape[:2]
    bs, ts = xs.shape[:2]
    n_hist = CONV_WIDTH - 1
    q_pos_s = PAST_LEN + jnp.arange(ts, dtype=jnp.int32)
    k_pos_s = jnp.arange(PAST_LEN + ts, dtype=jnp.int32)
    kp_l, vp_l, sp_l, cp_l = [], [], [], []
    ks_l, vs_l, ss_l, cs_l = [], [], [], []
    for l in range(DEPTH):
        h = rmsnorm(xp, norm1_w[l])
        conv_in, z, a, b, sq, sk, sv = split_cols(h @ w_in[l])
        cbuf = jnp.pad(conv_in, ((0, 0), (n_hist, 0), (0, 0)))
        q, k, v, g, beta = gdn_inputs(causal_conv(cbuf, conv_w[l], tpr), a, b, a_log[l], dt_bias[l])
        s0 = jnp.zeros((bp, GDN_HEADS, GDN_DK, GDN_DV), jnp.float32)
        o_meta, s_meta = gdn_chunked(q[:, :N_META], k[:, :N_META], v[:, :N_META],
                                     g[:, :N_META], beta[:, :N_META], s0, N_META)
        o_real, s_real = gdn_chunked(q[:, N_META:], k[:, N_META:], v[:, N_META:],
                                     g[:, N_META:], beta[:, N_META:], s_meta, GDN_CHUNK)
        o_a = gdn_output(jnp.concatenate([o_meta, o_real], axis=1), z, gdn_norm_w[l], dt)
        qb, kb, vb = sb_heads(sq, sk, sv, q_norm_w[l], k_norm_w[l])
        o_b = sb_prompt(qb, kb, vb, sb_bias[l]).astype(dt)
        xp = xp + jnp.concatenate([o_a, o_b], axis=-1) @ w_o[l]
        xp = xp + sq_relu_mlp(rmsnorm(xp, norm2_w[l]), w_up[l], w_down[l])
        kp_l.append(kb)
        vp_l.append(vb)
        sp_l.append(s_real.astype(state_gdn.dtype))
        cp_l.append(cbuf[:, -n_hist:].astype(state_conv.dtype))

        h = rmsnorm(xs, norm1_w[l])
        conv_in, z, a, b, sq, sk, sv = split_cols(h @ w_in[l])
        cbuf = jnp.concatenate([state_conv[l].astype(conv_in.dtype), conv_in], axis=1)
        q, k, v, g, beta = gdn_inputs(causal_conv(cbuf, conv_w[l], ts), a, b, a_log[l], dt_bias[l])
        o_rec, s_new = gdn_recurrent(q, k, v, g, beta, state_gdn[l].astype(jnp.float32))
        o_a = gdn_output(o_rec, z, gdn_norm_w[l], dt)
        qb, kb, vb = sb_heads(sq, sk, sv, q_norm_w[l], k_norm_w[l])
        past_k = cache_k[l][page_table].reshape(bs, PAST_LEN, SB_HEADS, SB_DH)
        past_v = cache_v[l][page_table].reshape(bs, PAST_LEN, SB_HEADS, SB_DH)
        k_all = jnp.concatenate([past_k.astype(kb.dtype), kb], axis=1)
        v_all = jnp.concatenate([past_v.astype(vb.dtype), vb], axis=1)
        o_b = sb_attend(qb, k_all, v_all, q_pos_s, k_pos_s, sb_bias[l]).reshape(bs, ts, SB_WIDTH).astype(dt)
        xs = xs + jnp.concatenate([o_a, o_b], axis=-1) @ w_o[l]
        xs = xs + sq_relu_mlp(rmsnorm(xs, norm2_w[l]), w_up[l], w_down[l])
        ks_l.append(kb)
        vs_l.append(vb)
        ss_l.append(s_new.astype(state_gdn.dtype))
        cs_l.append(cbuf[:, -n_hist:].astype(state_conv.dtype))

    y_prompt = xp[:, N_META:]
    return (y_prompt, xs,
            jnp.stack(kp_l), jnp.stack(vp_l), jnp.stack(sp_l), jnp.stack(cp_l),
            jnp.stack(ks_l), jnp.stack(vs_l), jnp.stack(ss_l), jnp.stack(cs_l))
```

```python
import functools

import jax
import jax.numpy as jnp
from jax import lax
from jax.experimental import pallas as pl
from jax.experimental.pallas import tpu as pltpu

F32 = jnp.float32
BF16 = jnp.bfloat16
EPS = 1e-6

N_META = 16
GDN_HEADS = 4
GDN_D = 128
GDN_WIDTH = GDN_HEADS * GDN_D
CONV_DIM = 3 * GDN_WIDTH
CONV_TAPS = 4
GDN_CHUNK = 64
SB_HEADS = 8
SB_DH = 64
SB_WIDTH = SB_HEADS * SB_DH
AB_COLS = 128
C_CONV = 0
C_Z = C_CONV + CONV_DIM
C_SQ = C_Z + GDN_WIDTH
C_SK = C_SQ + SB_WIDTH
C_SV = C_SK + SB_WIDTH
C_AB = C_SV + SB_WIDTH
IN_COLS_PAD = C_AB + AB_COLS

ROW_TILE = 256
GDN_TILE = 2 * GDN_CHUNK
VMEM_LIMIT = 56 * 1024 * 1024


def _cparams(sem):
    return pltpu.CompilerParams(dimension_semantics=sem, vmem_limit_bytes=VMEM_LIMIT)


def _dot(a, b):
    return jnp.dot(a, b, preferred_element_type=F32)


def _dot_nt(a, b):
    return lax.dot_general(a, b, (((1,), (1,)), ((), ())), preferred_element_type=F32)


def _split2(x):
    hi = x.astype(BF16)
    lo = (x - hi.astype(F32)).astype(BF16)
    return hi, lo


def _split3(x):
    hi = x.astype(BF16)
    r = x - hi.astype(F32)
    mid = r.astype(BF16)
    lo = (r - mid.astype(F32)).astype(BF16)
    return hi, mid, lo


def _dot3(a, b):
    ah, al = _split2(a)
    bh, bl = _split2(b)
    return _dot(ah, bh) + (_dot(ah, bl) + _dot(al, bh))


def _sigmoid(x):
    return 1.0 / (1.0 + jnp.exp(-x))


def _softplus(x):
    return jnp.maximum(x, 0.0) + jnp.log(1.0 + jnp.exp(-jnp.abs(x)))


def _inproj_kernel(x_ref, nw_ref, w_ref, gseg_ref, qnw_ref, knw_ref, *outs, prompt):
    x = x_ref[0]
    ms = jnp.mean(x * x, axis=-1, keepdims=True)
    h = (x * lax.rsqrt(ms + EPS) * nw_ref[...]).astype(BF16)

    def proj(lo, hi):
        return _dot(h, w_ref[:, lo:hi])

    def headnorm(s, wrow):
        hi, lo = _split2(s * s)
        ss = _dot(hi, gseg_ref[...]) + _dot(lo, gseg_ref[...])
        return s * lax.rsqrt(ss * (1.0 / SB_DH) + EPS) * wrow

    qn = headnorm(proj(C_SQ, C_SK), qnw_ref[...]) * (SB_DH ** -0.5)
    kn = headnorm(proj(C_SK, C_SV), knw_ref[...])
    sv = proj(C_SV, C_AB)
    if prompt:
        conv_ref, z_ref, ab_ref, q_ref, kt_ref, v_ref, knew_ref, vnew_ref = outs
        for hd in range(SB_HEADS):
            q_ref[0, hd] = qn[:, hd * SB_DH:(hd + 1) * SB_DH].astype(BF16)
            v_ref[0, hd] = sv[:, hd * SB_DH:(hd + 1) * SB_DH].astype(BF16)
        kt_ref[0] = kn.T.astype(BF16)
    else:
        conv_ref, z_ref, ab_ref, q_ref, knew_ref, vnew_ref = outs
        q_ref[0] = qn
    conv_ref[0] = proj(C_CONV, C_Z)
    z_ref[0] = proj(C_Z, C_SQ)
    ab_ref[0] = proj(C_AB, IN_COLS_PAD)
    knew_ref[0] = kn
    vnew_ref[0] = sv


def _inproj(x, nw, w, gseg, qnw, knw, *, prompt, tm):
    b, t, d = x.shape
    grid = (b, t // tm)
    row = lambda i, j: (i, j, 0)
    const = lambda i, j: (0, 0)
    in_specs = [
        pl.BlockSpec((1, tm, d), row),
        pl.BlockSpec((1, d), const),
        pl.BlockSpec((d, IN_COLS_PAD), const),
        pl.BlockSpec((SB_WIDTH, SB_WIDTH), const),
        pl.BlockSpec((1, SB_WIDTH), const),
        pl.BlockSpec((1, SB_WIDTH), const),
    ]
    f32_rows = lambda n: (jax.ShapeDtypeStruct((b, t, n), F32), pl.BlockSpec((1, tm, n), row))
    outs = [f32_rows(CONV_DIM), f32_rows(GDN_WIDTH), f32_rows(AB_COLS)]
    if prompt:
        hm = lambda i, j: (i, 0, j, 0)
        outs += [
            (jax.ShapeDtypeStruct((b, SB_HEADS, t, SB_DH), BF16), pl.BlockSpec((1, SB_HEADS, tm, SB_DH), hm)),
            (jax.ShapeDtypeStruct((b, SB_WIDTH, t), BF16), pl.BlockSpec((1, SB_WIDTH, tm), lambda i, j: (i, 0, j))),
            (jax.ShapeDtypeStruct((b, SB_HEADS, t, SB_DH), BF16), pl.BlockSpec((1, SB_HEADS, tm, SB_DH), hm)),
        ]
    else:
        outs += [f32_rows(SB_WIDTH)]
    outs += [f32_rows(SB_WIDTH), f32_rows(SB_WIDTH)]
    return pl.pallas_call(
        functools.partial(_inproj_kernel, prompt=prompt),
        name="inproj_prompt" if prompt else "inproj_sample",
        grid=grid,
        in_specs=in_specs,
        out_specs=[o[1] for o in outs],
        out_shape=[o[0] for o in outs],
        compiler_params=_cparams(("parallel", "parallel")),
    )(x, nw, w, gseg, qnw, knw)


def _gdn_inputs(conv_out, ab, pa, pb, valid):
    u = conv_out * _sigmoid(conv_out)
    parts = []
    for hd in range(2 * GDN_HEADS):
        sl = u[:, hd * GDN_D:(hd + 1) * GDN_D]
        nrm = sl * lax.rsqrt(jnp.sum(sl * sl, axis=-1, keepdims=True) + EPS)
        parts.append(nrm * (GDN_D ** -0.5) if hd < GDN_HEADS else nrm)
    parts.append(u[:, 2 * GDN_WIDTH:])
    g = -jnp.exp(pa) * _softplus(ab + pb)
    beta = _sigmoid(ab)
    lane = lax.broadcasted_iota(jnp.int32, ab.shape, 1)
    gb = jnp.where(lane < GDN_HEADS, g, jnp.where(lane < 2 * GDN_HEADS, beta, 0.0))
    if valid is not None:
        gb = jnp.where(valid, gb, 0.0)
    return jnp.concatenate(parts, axis=1), gb


def _conv_prompt_kernel(x_ref, prev_ref, cw_ref, ab_ref, pa_ref, pb_ref, u_ref, gb_ref, gbt_ref, xs_ref, *, t_real):
    i = pl.program_id(1)
    tm = x_ref.shape[1]
    xs_ref[0:8] = jnp.where(i > 0, prev_ref[0], 0.0)
    xs_ref[8:] = x_ref[0]
    out = xs_ref[pl.ds(5, tm)] * cw_ref[0:1]
    for tap in range(1, CONV_TAPS):
        out = out + xs_ref[pl.ds(5 + tap, tm)] * cw_ref[tap:tap + 1]
    row = i * tm + lax.broadcasted_iota(jnp.int32, (tm, AB_COLS), 0)
    u, gb = _gdn_inputs(out, ab_ref[0], pa_ref[...], pb_ref[...], row < t_real)
    u_ref[0] = u
    gb_ref[0] = gb
    gbt_ref[0] = gb.T[0:8]


def _conv_prompt(conv_in, cw, ab, pa, pb, *, t_real, tm):
    b, t, c = conv_in.shape
    row = lambda i, j: (i, j, 0)
    const = lambda i, j: (0, 0)
    return pl.pallas_call(
        functools.partial(_conv_prompt_kernel, t_real=t_real),
        name="conv_prompt",
        grid=(b, t // tm),
        in_specs=[
            pl.BlockSpec((1, tm, c), row),
            pl.BlockSpec((1, 8, c), lambda i, j: (i, jnp.maximum(j * (tm // 8) - 1, 0), 0)),
            pl.BlockSpec((CONV_TAPS, c), const),
            pl.BlockSpec((1, tm, AB_COLS), row),
            pl.BlockSpec((1, AB_COLS), const),
            pl.BlockSpec((1, AB_COLS), const),
        ],
        out_specs=[
            pl.BlockSpec((1, tm, c), row),
            pl.BlockSpec((1, tm, AB_COLS), row),
            pl.BlockSpec((1, 8, tm), lambda i, j: (i, 0, j)),
        ],
        out_shape=[
            jax.ShapeDtypeStruct((b, t, c), F32),
            jax.ShapeDtypeStruct((b, t, AB_COLS), F32),
            jax.ShapeDtypeStruct((b, 8, t), F32),
        ],
        scratch_shapes=[pltpu.VMEM((tm + 8, c), F32)],
        compiler_params=_cparams(("parallel", "parallel")),
    )(conv_in, conv_in, cw, ab, pa, pb)


def _conv_sample_kernel(x_ref, st_ref, cw_ref, ab_ref, pa_ref, pb_ref, u_ref, gb_ref, ns_ref):
    c = x_ref.shape[1]
    taps = [st_ref[:, k * c:(k + 1) * c] for k in range(CONV_TAPS - 1)] + [x_ref[...]]
    out = taps[0] * cw_ref[0:1]
    for tap in range(1, CONV_TAPS):
        out = out + taps[tap] * cw_ref[tap:tap + 1]
    u, gb = _gdn_inputs(out, ab_ref[...], pa_ref[...], pb_ref[...], None)
    u_ref[...] = u
    gb_ref[...] = gb
    for k in range(CONV_TAPS - 1):
        ns_ref[:, k * c:(k + 1) * c] = taps[k + 1]


def _conv_sample(conv_in, state, cw, ab, pa, pb):
    n, c = conv_in.shape
    return pl.pallas_call(
        _conv_sample_kernel,
        name="conv_sample",
        out_shape=[
            jax.ShapeDtypeStruct((n, c), F32),
            jax.ShapeDtypeStruct((n, AB_COLS), F32),
            jax.ShapeDtypeStruct((n, (CONV_TAPS - 1) * c), F32),
        ],
        compiler_params=pltpu.CompilerParams(vmem_limit_bytes=VMEM_LIMIT),
    )(conv_in, state, cw, ab, pa, pb)


def _gated_norm(o, z, nw):
    return o * lax.rsqrt(jnp.mean(o * o, axis=-1, keepdims=True) + EPS) * nw * (z * _sigmoid(z))


def _gdn_chunk_kernel(u_ref, gb_ref, gbt_ref, z_ref, nw_ref, ltri_ref, ltrit_ref, o_ref, sout_ref, s_ref):
    step = pl.program_id(1)
    n = GDN_TILE
    c = GDN_CHUNK

    @pl.when(step == 0)
    def _():
        s_ref[...] = jnp.zeros_like(s_ref)

    gb = gb_ref[0]
    g1, g2, g3 = _split3(gb)
    ltri = ltri_ref[...]
    cum_cols = _dot(ltri, g1) + (_dot(ltri, g2) + _dot(ltri, g3))
    t1, t2, t3 = _split3(gbt_ref[0])
    ltrit = ltrit_ref[...]
    cum_rows = _dot(t1, ltrit) + (_dot(t2, ltrit) + _dot(t3, ltrit))

    row = lax.broadcasted_iota(jnp.int32, (n, n), 0)
    col = lax.broadcasted_iota(jnp.int32, (n, n), 1)
    same = (row >= c) == (col >= c)
    causal = same & (row >= col)
    strict = same & (row > col)
    eye = (row == col).astype(F32)
    second = lax.broadcasted_iota(jnp.int32, (n, 1), 0) >= c
    zeros_c = jnp.zeros((c, GDN_D), F32)

    for hd in range(GDN_HEADS):
        q = u_ref[0, :, hd * GDN_D:(hd + 1) * GDN_D]
        k = u_ref[0, :, GDN_WIDTH + hd * GDN_D:GDN_WIDTH + (hd + 1) * GDN_D]
        v = u_ref[0, :, 2 * GDN_WIDTH + hd * GDN_D:2 * GDN_WIDTH + (hd + 1) * GDN_D]
        cc = cum_cols[:, hd:hd + 1]
        cr = cum_rows[hd:hd + 1, :]
        beta = gb[:, GDN_HEADS + hd:GDN_HEADS + hd + 1]
        gam = jnp.where(causal, jnp.exp(jnp.where(causal, cc - cr, 0.0)), 0.0)
        kb = k * beta
        k16 = k.astype(BF16)
        a_mat = jnp.where(strict, _dot_nt(kb.astype(BF16), k16) * gam, 0.0)
        qk = _dot_nt(q.astype(BF16), k16) * gam
        inv = eye - a_mat
        pw = _dot3(a_mat, a_mat)
        for _ in range(4):
            both = _dot3(jnp.concatenate([inv, pw], axis=0), pw)
            inv = inv + both[:n]
            pw = both[n:]
        inv = inv + _dot3(inv, pw)
        ecum = jnp.exp(cc)
        uw = _dot3(inv, jnp.concatenate([v * beta, kb * ecum], axis=1))
        uu = uw[:, :GDN_D]
        ww = uw[:, GDN_D:].astype(BF16)
        qg = (q * ecum).astype(BF16)
        g_last0 = cc[c - 1:c]
        g_last1 = cc[n - 1:n]
        g_last = jnp.where(second, g_last1, g_last0)
        kdt = (k * jnp.exp(g_last - cc)).T.astype(BF16)

        s0 = s_ref[hd]
        s0b = s0.astype(BF16)
        vn0 = uu[:c] - _dot(ww[:c], s0b)
        oi0 = _dot(qg[:c], s0b)
        s1 = s0 * jnp.exp(g_last0) + _dot(kdt, jnp.concatenate([vn0, zeros_c], axis=0).astype(BF16))
        s1b = s1.astype(BF16)
        vn1 = uu[c:] - _dot(ww[c:], s1b)
        oi1 = _dot(qg[c:], s1b)
        s2 = s1 * jnp.exp(g_last1) + _dot(kdt, jnp.concatenate([zeros_c, vn1], axis=0).astype(BF16))
        s_ref[hd] = s2
        vn = jnp.concatenate([vn0, vn1], axis=0).astype(BF16)
        o = jnp.concatenate([oi0, oi1], axis=0) + _dot(qk.astype(BF16), vn)
        zg = z_ref[0, :, hd * GDN_D:(hd + 1) * GDN_D]
        o_ref[0, :, hd * GDN_D:(hd + 1) * GDN_D] = _gated_norm(o, zg, nw_ref[...]).astype(BF16)

    @pl.when(step == pl.num_programs(1) - 1)
    def _():
        sout_ref[0] = s_ref[...]


def _gdn_prompt(u, gb, gbt, z, nw, ltri, ltrit):
    b, t, _ = u.shape
    n = GDN_TILE
    row = lambda i, j: (i, j, 0)
    const = lambda i, j: (0, 0)
    return pl.pallas_call(
        _gdn_chunk_kernel,
        name="gdn_prompt",
        grid=(b, t // n),
        in_specs=[
            pl.BlockSpec((1, n, CONV_DIM), row),
            pl.BlockSpec((1, n, AB_COLS), row),
            pl.BlockSpec((1, 8, n), lambda i, j: (i, 0, j)),
            pl.BlockSpec((1, n, GDN_WIDTH), row),
            pl.BlockSpec((1, GDN_D), const),
            pl.BlockSpec((n, n), const),
            pl.BlockSpec((n, n), const),
        ],
        out_specs=[
            pl.BlockSpec((1, n, GDN_WIDTH), row),
            pl.BlockSpec((1, GDN_HEADS, GDN_D, GDN_D), lambda i, j: (i, 0, 0, 0)),
        ],
        out_shape=[
            jax.ShapeDtypeStruct((b, t, GDN_WIDTH), BF16),
            jax.ShapeDtypeStruct((b, GDN_HEADS, GDN_D, GDN_D), F32),
        ],
        scratch_shapes=[pltpu.VMEM((GDN_HEADS, GDN_D, GDN_D), F32)],
        compiler_params=_cparams(("parallel", "arbitrary")),
    )(u, gb, gbt, z, nw, ltri, ltrit)


def _gdn_step_kernel(s_ref, u_ref, gb_ref, z_ref, nw_ref, sout_ref, o_ref):
    row = lax.broadcasted_iota(jnp.int32, (GDN_D, GDN_D), 0)
    col = lax.broadcasted_iota(jnp.int32, (GDN_D, GDN_D), 1)
    eye = row == col

    def column(r):
        return jnp.sum(jnp.where(eye, r, 0.0), axis=1, keepdims=True)

    for hd in range(GDN_HEADS):
        q = u_ref[0, :, hd * GDN_D:(hd + 1) * GDN_D]
        k = u_ref[0, :, GDN_WIDTH + hd * GDN_D:GDN_WIDTH + (hd + 1) * GDN_D]
        v = u_ref[0, :, 2 * GDN_WIDTH + hd * GDN_D:2 * GDN_WIDTH + (hd + 1) * GDN_D]
        g = gb_ref[0, :, hd:hd + 1]
        beta = gb_ref[0, :, GDN_HEADS + hd:GDN_HEADS + hd + 1]
        kc = column(k)
        s = s_ref[0, hd] * jnp.exp(g)
        err = v - jnp.sum(kc * s, axis=0, keepdims=True)
        s = s + kc * (err * beta)
        sout_ref[0, hd] = s
        o = jnp.sum(column(q) * s, axis=0, keepdims=True)
        zg = z_ref[0, :, hd * GDN_D:(hd + 1) * GDN_D]
        o_ref[0, :, hd * GDN_D:(hd + 1) * GDN_D] = _gated_norm(o, zg, nw_ref[...]).astype(BF16)


def _gdn_sample(state, u, gb, z, nw):
    n = state.shape[0]
    row = lambda i: (i, 0, 0)
    st = lambda i: (i, 0, 0, 0)
    return pl.pallas_call(
        _gdn_step_kernel,
        name="gdn_sample",
        grid=(n,),
        in_specs=[
            pl.BlockSpec((1, GDN_HEADS, GDN_D, GDN_D), st),
            pl.BlockSpec((1, 1, CONV_DIM), row),
            pl.BlockSpec((1, 1, AB_COLS), row),
            pl.BlockSpec((1, 1, GDN_WIDTH), row),
            pl.BlockSpec((1, GDN_D), lambda i: (0, 0)),
        ],
        out_specs=[
            pl.BlockSpec((1, GDN_HEADS, GDN_D, GDN_D), st),
            pl.BlockSpec((1, 1, GDN_WIDTH), row),
        ],
        out_shape=[
            jax.ShapeDtypeStruct(state.shape, F32),
            jax.ShapeDtypeStruct((n, 1, GDN_WIDTH), BF16),
        ],
        compiler_params=_cparams(("parallel",)),
    )(state, u.reshape(n, 1, CONV_DIM), gb.reshape(n, 1, AB_COLS), z.reshape(n, 1, GDN_WIDTH), nw)


def _sb_block(z, v, uext, carry, mask):
    nk = z.shape[1]
    sp = _softplus(z)
    log_stay = -sp
    if mask is not None:
        log_stay = jnp.where(mask, log_stay, 0.0)
    hi, lo = _split2(log_stay)
    scan = _dot(hi, uext) + _dot(lo, uext)
    between = scan[:, :nk] + jnp.concatenate([carry] * (nk // 128), axis=1)
    w = jnp.exp((z - sp) + between)
    if mask is not None:
        w = jnp.where(mask, w, 0.0)
    return _dot(w.astype(BF16), v), scan[:, nk:]


def _sb_prompt_kernel(bias_ref, q_ref, kt_ref, v_ref, uext_ref, o_ref, acc_ref, car_ref):
    hp = pl.program_id(1)
    i = pl.program_id(2)
    nq = q_ref.shape[2]
    row = lax.broadcasted_iota(jnp.int32, (nq, nq), 0)
    col = lax.broadcasted_iota(jnp.int32, (nq, nq), 1)
    diag_mask = col < row
    outs = []
    for hh in range(2):
        q = q_ref[0, hh]
        bias = bias_ref[2 * hp + hh]

        def block(start, carry, mask):
            kt = kt_ref[0, hh, :, pl.ds(start, nq)]
            v = v_ref[0, hh, pl.ds(start, nq), :]
            return _sb_block(_dot(q, kt) + bias, v, uext_ref[...], carry, mask)

        o, rs = block(pl.multiple_of(i * nq, nq), jnp.zeros((nq, 128), F32), diag_mask)
        acc_ref[...] = o
        car_ref[...] = rs

        def body(jj, _):
            start = pl.multiple_of((i - 1 - jj) * nq, nq)
            o, rs = block(start, car_ref[...], None)
            acc_ref[...] += o
            car_ref[...] += rs
            return 0

        lax.fori_loop(0, i, body, 0)
        outs.append(acc_ref[...])
    o_ref[0] = jnp.concatenate(outs, axis=1).astype(BF16)


def _sb_prompt(q, kt, v, bias, uext, *, tq):
    b, nh, t, dh = q.shape
    kt = kt.reshape(b, nh, dh, t)
    return pl.pallas_call(
        _sb_prompt_kernel,
        name="sb_prompt",
        grid_spec=pltpu.PrefetchScalarGridSpec(
            num_scalar_prefetch=0,
            grid=(b, nh // 2, t // tq),
            in_specs=[
                pl.BlockSpec(memory_space=pltpu.SMEM),
                pl.BlockSpec((1, 2, tq, dh), lambda i, h, j: (i, h, j, 0)),
                pl.BlockSpec((1, 2, dh, t), lambda i, h, j: (i, h, 0, 0)),
                pl.BlockSpec((1, 2, t, dh), lambda i, h, j: (i, h, 0, 0)),
                pl.BlockSpec((tq, tq + 128), lambda i, h, j: (0, 0)),
            ],
            out_specs=pl.BlockSpec((1, tq, 2 * dh), lambda i, h, j: (i, j, h)),
            scratch_shapes=[pltpu.VMEM((tq, dh), F32), pltpu.VMEM((tq, 128), F32)],
        ),
        out_shape=jax.ShapeDtypeStruct((b, t, nh * dh), BF16),
        compiler_params=_cparams(("parallel", "parallel", "arbitrary")),
    )(bias, q, kt, v, uext)


def _sb_decode_kernel(pt_ref, q_ref, bias_ref, uext_ref, *refs, n_pages):
    k_refs = refs[:n_pages]
    v_refs = refs[n_pages:2 * n_pages]
    o_ref = refs[2 * n_pages]
    head_of_lane = lax.broadcasted_iota(jnp.int32, (SB_HEADS, SB_WIDTH), 1) // SB_DH
    own = head_of_lane == lax.broadcasted_iota(jnp.int32, (SB_HEADS, SB_WIDTH), 0)
    qbd = jnp.where(own, q_ref[0], 0.0).astype(BF16)
    acc = jnp.zeros((SB_HEADS, SB_WIDTH), F32)
    carry = jnp.zeros((SB_HEADS, 128), F32)
    for p in range(n_pages - 1, -1, -1):
        z = _dot_nt(qbd, k_refs[p][0].astype(BF16)) + bias_ref[...]
        o, rs = _sb_block(z, v_refs[p][0].astype(BF16), uext_ref[...], carry, None)
        acc = acc + o
        carry = carry + rs
    o_ref[0] = jnp.sum(jnp.where(own, acc, 0.0), axis=0, keepdims=True).astype(BF16)


def _sb_decode(q, cache_k, cache_v, page_table, bias_b, uext, *, layer):
    n, n_pages = page_table.shape
    depth, n_pool, page, nh, dh = cache_k.shape
    ck = cache_k.reshape(depth * n_pool, page, nh * dh)
    cv = cache_v.reshape(depth * n_pool, page, nh * dh)

    def page_spec(p):
        return pl.BlockSpec((1, page, nh * dh), lambda i, pt: (layer * n_pool + pt[i * n_pages + p], 0, 0))

    row = lambda i, pt: (i, 0, 0)
    return pl.pallas_call(
        functools.partial(_sb_decode_kernel, n_pages=n_pages),
        name="sb_decode",
        grid_spec=pltpu.PrefetchScalarGridSpec(
            num_scalar_prefetch=1,
            grid=(n,),
            in_specs=[
                pl.BlockSpec((1, 1, nh * dh), row),
                pl.BlockSpec((nh, 128), lambda i, pt: (0, 0)),
                pl.BlockSpec((page, page + 128), lambda i, pt: (0, 0)),
            ] + [page_spec(p) for p in range(n_pages)] * 2,
            out_specs=pl.BlockSpec((1, 1, nh * dh), row),
        ),
        out_shape=jax.ShapeDtypeStruct((n, 1, nh * dh), BF16),
        compiler_params=_cparams(("parallel",)),
    )(page_table.reshape(-1), q.reshape(n, 1, nh * dh), bias_b, uext, *([ck] * n_pages), *([cv] * n_pages))


def _mlp_kernel(x_ref, oa_ref, ob_ref, wo_ref, nw_ref, wup_ref, wdn_ref, y_ref, x1_ref, xn_ref, acc_ref):
    j = pl.program_id(1)

    @pl.when(j == 0)
    def _():
        x1 = x_ref[...] + (_dot(oa_ref[...], wo_ref[:GDN_WIDTH]) + _dot(ob_ref[...], wo_ref[GDN_WIDTH:]))
        x1_ref[...] = x1
        ms = jnp.mean(x1 * x1, axis=-1, keepdims=True)
        xn_ref[...] = (x1 * lax.rsqrt(ms + EPS) * nw_ref[...]).astype(BF16)
        acc_ref[...] = jnp.zeros_like(acc_ref)

    h = jnp.maximum(_dot(xn_ref[...], wup_ref[...]), 0.0)
    acc_ref[...] += _dot((h * h).astype(BF16), wdn_ref[...])

    @pl.when(j == pl.num_programs(1) - 1)
    def _():
        y_ref[...] = x1_ref[...] + acc_ref[...]


def _mix_mlp(x, oa, ob, wo, nw, wup, wdn, *, tm, tf):
    m, d = x.shape
    ff = wup.shape[1]
    row = lambda i, j: (i, 0)
    const = lambda i, j: (0, 0)
    return pl.pallas_call(
        _mlp_kernel,
        name="mix_mlp",
        grid=(m // tm, ff // tf),
        in_specs=[
            pl.BlockSpec((tm, d), row),
            pl.BlockSpec((tm, GDN_WIDTH), row),
            pl.BlockSpec((tm, SB_WIDTH), row),
            pl.BlockSpec((d, d), const),
            pl.BlockSpec((1, d), const),
            pl.BlockSpec((d, tf), lambda i, j: (0, j)),
            pl.BlockSpec((tf, d), lambda i, j: (j, 0)),
        ],
        out_specs=pl.BlockSpec((tm, d), row),
        out_shape=jax.ShapeDtypeStruct((m, d), F32),
        scratch_shapes=[pltpu.VMEM((tm, d), F32), pltpu.VMEM((tm, d), BF16), pltpu.VMEM((tm, d), F32)],
        compiler_params=_cparams(("parallel", "arbitrary")),
    )(x, oa, ob, wo, nw, wup, wdn)


def _tri_consts(n_chunk_tile, chunk, tq, page):
    r = jnp.arange(n_chunk_tile)
    same = (r[:, None] // chunk) == (r[None, :] // chunk)
    ltri = (same & (r[None, :] <= r[:, None])).astype(BF16)

    def uext(nk):
        k = jnp.arange(nk)
        later = (k[:, None] > k[None, :]).astype(BF16)
        return jnp.concatenate([later, jnp.ones((nk, 128), BF16)], axis=1)

    return ltri, ltri.T, uext(tq), uext(page)


def kernel(x_prompt, x_sample, cache_k, cache_v, page_table, state_gdn, state_conv, meta_tokens, norm1_w, w_in,
           conv_w, a_log, dt_bias, gdn_norm_w, q_norm_w, k_norm_w, sb_bias, w_o, norm2_w, w_up, w_down):
    depth, d_model = norm1_w.shape
    bp, seq, _ = x_prompt.shape
    bs = x_sample.shape[0]
    page = cache_k.shape[2]
    t_real = N_META + seq
    tp = -(-t_real // ROW_TILE) * ROW_TILE

    c0 = CONV_DIM
    c1 = c0 + GDN_WIDTH
    c2 = c1 + GDN_HEADS
    c3 = c2 + GDN_HEADS
    w_in_r = jnp.concatenate(
        [w_in[..., :c1], w_in[..., c3:], w_in[..., c1:c3],
         jnp.zeros((depth, d_model, AB_COLS - 2 * GDN_HEADS), w_in.dtype)], axis=-1).astype(BF16)
    w_o16, w_up16, w_dn16 = w_o.astype(BF16), w_up.astype(BF16), w_down.astype(BF16)
    pad_lanes = lambda a: jnp.pad(a, ((0, 0), (0, AB_COLS - a.shape[1])))[:, None, :]
    pa, pb = pad_lanes(a_log), pad_lanes(dt_bias)
    qnw = jnp.tile(q_norm_w, (1, SB_HEADS))[:, None, :]
    knw = jnp.tile(k_norm_w, (1, SB_HEADS))[:, None, :]
    lane_head = jnp.arange(SB_WIDTH) // SB_DH
    gseg = (lane_head[:, None] == lane_head[None, :]).astype(BF16)
    ltri, ltrit, uext_q, uext_p = _tri_consts(GDN_TILE, GDN_CHUNK, ROW_TILE, page)
    bias_b = jnp.broadcast_to(sb_bias[:, :, None], (depth, SB_HEADS, 128))

    meta = jnp.broadcast_to(meta_tokens[None], (bp, N_META, d_model))
    xp = jnp.concatenate([meta, x_prompt, jnp.zeros((bp, tp - t_real, d_model), x_prompt.dtype)], axis=1)
    xs = x_sample.reshape(1, bs, d_model)
    st_conv = state_conv.reshape(depth, bs, (CONV_TAPS - 1) * CONV_DIM)

    kp_l, vp_l, sp_l, cp_l = [], [], [], []
    ks_l, vs_l, ss_l, cs_l = [], [], [], []
    for l in range(depth):
        n1, n2 = norm1_w[l][None], norm2_w[l][None]
        gnw = gdn_norm_w[l][None]
        conv_in, z, ab, q, kt, v, knew, vnew = _inproj(xp, n1, w_in_r[l], gseg, qnw[l], knw[l], prompt=True,
                                                      tm=ROW_TILE)
        u, gb, gbt = _conv_prompt(conv_in, conv_w[l], ab, pa[l], pb[l], t_real=t_real, tm=ROW_TILE)
        o_a, s_fin = _gdn_prompt(u, gb, gbt, z, gnw, ltri, ltrit)
        o_b = _sb_prompt(q, kt, v, sb_bias[l], uext_q, tq=ROW_TILE)
        xp = _mix_mlp(xp.reshape(bp * tp, d_model), o_a.reshape(bp * tp, GDN_WIDTH), o_b.reshape(bp * tp, SB_WIDTH),
                      w_o16[l], n2, w_up16[l], w_dn16[l], tm=512, tf=1024).reshape(bp, tp, d_model)
        kp_l.append(knew[:, :t_real].reshape(bp, t_real, SB_HEADS, SB_DH))
        vp_l.append(vnew[:, :t_real].reshape(bp, t_real, SB_HEADS, SB_DH))
        sp_l.append(s_fin)
        cp_l.append(conv_in[:, t_real - (CONV_TAPS - 1):t_real])

        conv_in, z, ab, q, knew, vnew = _inproj(xs, n1, w_in_r[l], gseg, qnw[l], knw[l], prompt=False, tm=bs)
        u, gb, ns = _conv_sample(conv_in[0], st_conv[l], conv_w[l], ab[0], pa[l], pb[l])
        s_new, o_a = _gdn_sample(state_gdn[l], u, gb, z[0], gnw)
        o_b = _sb_decode(q[0], cache_k, cache_v, page_table, bias_b[l], uext_p, layer=l)
        xs = _mix_mlp(xs[0], o_a.reshape(bs, GDN_WIDTH), o_b.reshape(bs, SB_WIDTH), w_o16[l], n2, w_up16[l],
                      w_dn16[l], tm=bs, tf=1024)[None]
        ks_l.append(knew.reshape(bs, 1, SB_HEADS, SB_DH))
        vs_l.append(vnew.reshape(bs, 1, SB_HEADS, SB_DH))
        ss_l.append(s_new)
        cs_l.append(ns.reshape(bs, CONV_TAPS - 1, CONV_DIM))

    y_prompt = xp[:, N_META:t_real]
    return (y_prompt, xs.reshape(bs, 1, d_model),
            jnp.stack(kp_l), jnp.stack(vp_l), jnp.stack(sp_l), jnp.stack(cp_l),
            jnp.stack(ks_l), jnp.stack(vs_l), jnp.stack(ss_l), jnp.stack(cs_l))
```

```python
import functools

import jax
import jax.numpy as jnp
from jax import lax
from jax.experimental import pallas as pl
from jax.experimental.pallas import tpu as pltpu

F32 = jnp.float32
BF16 = jnp.bfloat16
EPS = 1e-6
LOG2E = 1.4426950408889634

N_META = 16
GDN_HEADS = 4
GDN_D = 128
GDN_WIDTH = GDN_HEADS * GDN_D
CONV_DIM = 3 * GDN_WIDTH
CONV_TAPS = 4
GDN_CHUNK = 64
SB_HEADS = 8
SB_DH = 64
SB_WIDTH = SB_HEADS * SB_DH
AB_COLS = 128
C_CONV = 0
C_Z = C_CONV + CONV_DIM
C_SQ = C_Z + GDN_WIDTH
C_SK = C_SQ + SB_WIDTH
C_SV = C_SK + SB_WIDTH
C_AB = C_SV + SB_WIDTH
IN_COLS_PAD = C_AB + AB_COLS

ROW_TILE = 256
GDN_TILE = 2 * GDN_CHUNK
VMEM_LIMIT = 56 * 1024 * 1024


def _cparams(sem):
    return pltpu.CompilerParams(dimension_semantics=sem, vmem_limit_bytes=VMEM_LIMIT)


def _dot(a, b):
    return jnp.dot(a, b, preferred_element_type=F32)


def _dot_nt(a, b):
    return lax.dot_general(a, b, (((1,), (1,)), ((), ())), preferred_element_type=F32)


def _split2(x):
    hi = x.astype(BF16)
    lo = (x - hi.astype(F32)).astype(BF16)
    return hi, lo


def _split3(x):
    hi = x.astype(BF16)
    r = x - hi.astype(F32)
    mid = r.astype(BF16)
    lo = (r - mid.astype(F32)).astype(BF16)
    return hi, mid, lo


def _dot3(a, b):
    ah, al = _split2(a)
    bh, bl = _split2(b)
    return _dot(ah, bh) + (_dot(ah, bl) + _dot(al, bh))


def _sigmoid(x):
    return 1.0 / (1.0 + jnp.exp(-x))


def _softplus(x):
    return jnp.maximum(x, 0.0) + jnp.log(1.0 + jnp.exp(-jnp.abs(x)))


def _inproj_kernel(x_ref, nw_ref, w_ref, gseg_ref, qnw_ref, knw_ref, *outs, prompt):
    x = x_ref[0]
    ms = jnp.mean(x * x, axis=-1, keepdims=True)
    h = (x * lax.rsqrt(ms + EPS) * nw_ref[...]).astype(BF16)

    def proj(lo, hi):
        return _dot(h, w_ref[:, lo:hi])

    def headnorm(s, wrow):
        hi, lo = _split2(s * s)
        ss = _dot(hi, gseg_ref[...]) + _dot(lo, gseg_ref[...])
        return s * lax.rsqrt(ss * (1.0 / SB_DH) + EPS) * wrow

    qn = headnorm(proj(C_SQ, C_SK), qnw_ref[...]) * (-(SB_DH ** -0.5) * LOG2E)
    kn = headnorm(proj(C_SK, C_SV), knw_ref[...])
    sv = proj(C_SV, C_AB)
    if prompt:
        conv_ref, z_ref, ab_ref, q_ref, kt_ref, v_ref, knew_ref, vnew_ref = outs
        for hd in range(SB_HEADS):
            q_ref[0, hd] = qn[:, hd * SB_DH:(hd + 1) * SB_DH].astype(BF16)
            v_ref[0, hd] = sv[:, hd * SB_DH:(hd + 1) * SB_DH].astype(BF16)
        kt_ref[0] = kn.T.astype(BF16)
    else:
        conv_ref, z_ref, ab_ref, q_ref, knew_ref, vnew_ref = outs
        q_ref[0] = qn
    conv_ref[0] = proj(C_CONV, C_Z)
    z_ref[0] = proj(C_Z, C_SQ)
    ab_ref[0] = proj(C_AB, IN_COLS_PAD)
    knew_ref[0] = kn
    vnew_ref[0] = sv


def _inproj(x, nw, w, gseg, qnw, knw, *, prompt, tm):
    b, t, d = x.shape
    grid = (b, t // tm)
    row = lambda i, j: (i, j, 0)
    const = lambda i, j: (0, 0)
    in_specs = [
        pl.BlockSpec((1, tm, d), row),
        pl.BlockSpec((1, d), const),
        pl.BlockSpec((d, IN_COLS_PAD), const),
        pl.BlockSpec((SB_WIDTH, SB_WIDTH), const),
        pl.BlockSpec((1, SB_WIDTH), const),
        pl.BlockSpec((1, SB_WIDTH), const),
    ]
    f32_rows = lambda n: (jax.ShapeDtypeStruct((b, t, n), F32), pl.BlockSpec((1, tm, n), row))
    outs = [f32_rows(CONV_DIM), f32_rows(GDN_WIDTH), f32_rows(AB_COLS)]
    if prompt:
        hm = lambda i, j: (i, 0, j, 0)
        outs += [
            (jax.ShapeDtypeStruct((b, SB_HEADS, t, SB_DH), BF16), pl.BlockSpec((1, SB_HEADS, tm, SB_DH), hm)),
            (jax.ShapeDtypeStruct((b, SB_WIDTH, t), BF16), pl.BlockSpec((1, SB_WIDTH, tm), lambda i, j: (i, 0, j))),
            (jax.ShapeDtypeStruct((b, SB_HEADS, t, SB_DH), BF16), pl.BlockSpec((1, SB_HEADS, tm, SB_DH), hm)),
        ]
    else:
        outs += [f32_rows(SB_WIDTH)]
    outs += [f32_rows(SB_WIDTH), f32_rows(SB_WIDTH)]
    return pl.pallas_call(
        functools.partial(_inproj_kernel, prompt=prompt),
        name="inproj_prompt" if prompt else "inproj_sample",
        grid=grid,
        in_specs=in_specs,
        out_specs=[o[1] for o in outs],
        out_shape=[o[0] for o in outs],
        compiler_params=_cparams(("parallel", "parallel")),
    )(x, nw, w, gseg, qnw, knw)


def _gdn_inputs(conv_out, ab, pa, pb, valid):
    u = conv_out * _sigmoid(conv_out)
    parts = []
    for hd in range(2 * GDN_HEADS):
        sl = u[:, hd * GDN_D:(hd + 1) * GDN_D]
        nrm = sl * lax.rsqrt(jnp.sum(sl * sl, axis=-1, keepdims=True) + EPS)
        parts.append(nrm * (GDN_D ** -0.5) if hd < GDN_HEADS else nrm)
    parts.append(u[:, 2 * GDN_WIDTH:])
    g = -jnp.exp(pa) * _softplus(ab + pb)
    beta = _sigmoid(ab)
    lane = lax.broadcasted_iota(jnp.int32, ab.shape, 1)
    gb = jnp.where(lane < GDN_HEADS, g, jnp.where(lane < 2 * GDN_HEADS, beta, 0.0))
    if valid is not None:
        gb = jnp.where(valid, gb, 0.0)
    return jnp.concatenate(parts, axis=1), gb


def _conv_prompt_kernel(x_ref, prev_ref, cw_ref, ab_ref, pa_ref, pb_ref, u_ref, gb_ref, gbt_ref, xs_ref, *, t_real):
    i = pl.program_id(1)
    tm = x_ref.shape[1]
    xs_ref[0:8] = jnp.where(i > 0, prev_ref[0], 0.0)
    xs_ref[8:] = x_ref[0]
    out = xs_ref[pl.ds(5, tm)] * cw_ref[0:1]
    for tap in range(1, CONV_TAPS):
        out = out + xs_ref[pl.ds(5 + tap, tm)] * cw_ref[tap:tap + 1]
    row = i * tm + lax.broadcasted_iota(jnp.int32, (tm, AB_COLS), 0)
    u, gb = _gdn_inputs(out, ab_ref[0], pa_ref[...], pb_ref[...], row < t_real)
    u_ref[0] = u
    gb_ref[0] = gb
    gbt_ref[0] = gb.T[0:8]


def _conv_prompt(conv_in, cw, ab, pa, pb, *, t_real, tm):
    b, t, c = conv_in.shape
    row = lambda i, j: (i, j, 0)
    const = lambda i, j: (0, 0)
    return pl.pallas_call(
        functools.partial(_conv_prompt_kernel, t_real=t_real),
        name="conv_prompt",
        grid=(b, t // tm),
        in_specs=[
            pl.BlockSpec((1, tm, c), row),
            pl.BlockSpec((1, 8, c), lambda i, j: (i, jnp.maximum(j * (tm // 8) - 1, 0), 0)),
            pl.BlockSpec((CONV_TAPS, c), const),
            pl.BlockSpec((1, tm, AB_COLS), row),
            pl.BlockSpec((1, AB_COLS), const),
            pl.BlockSpec((1, AB_COLS), const),
        ],
        out_specs=[
            pl.BlockSpec((1, tm, c), row),
            pl.BlockSpec((1, tm, AB_COLS), row),
            pl.BlockSpec((1, 8, tm), lambda i, j: (i, 0, j)),
        ],
        out_shape=[
            jax.ShapeDtypeStruct((b, t, c), F32),
            jax.ShapeDtypeStruct((b, t, AB_COLS), F32),
            jax.ShapeDtypeStruct((b, 8, t), F32),
        ],
        scratch_shapes=[pltpu.VMEM((tm + 8, c), F32)],
        compiler_params=_cparams(("parallel", "parallel")),
    )(conv_in, conv_in, cw, ab, pa, pb)


def _conv_sample_kernel(x_ref, st_ref, cw_ref, ab_ref, pa_ref, pb_ref, u_ref, gb_ref, ns_ref):
    c = x_ref.shape[1]
    taps = [st_ref[:, k * c:(k + 1) * c] for k in range(CONV_TAPS - 1)] + [x_ref[...]]
    out = taps[0] * cw_ref[0:1]
    for tap in range(1, CONV_TAPS):
        out = out + taps[tap] * cw_ref[tap:tap + 1]
    u, gb = _gdn_inputs(out, ab_ref[...], pa_ref[...], pb_ref[...], None)
    u_ref[...] = u
    gb_ref[...] = gb
    for k in range(CONV_TAPS - 1):
        ns_ref[:, k * c:(k + 1) * c] = taps[k + 1]


def _conv_sample(conv_in, state, cw, ab, pa, pb):
    n, c = conv_in.shape
    return pl.pallas_call(
        _conv_sample_kernel,
        name="conv_sample",
        out_shape=[
            jax.ShapeDtypeStruct((n, c), F32),
            jax.ShapeDtypeStruct((n, AB_COLS), F32),
            jax.ShapeDtypeStruct((n, (CONV_TAPS - 1) * c), F32),
        ],
        compiler_params=pltpu.CompilerParams(vmem_limit_bytes=VMEM_LIMIT),
    )(conv_in, state, cw, ab, pa, pb)


def _gated_norm(o, z, nw):
    return o * lax.rsqrt(jnp.mean(o * o, axis=-1, keepdims=True) + EPS) * nw * (z * _sigmoid(z))


def _gdn_chunk_kernel(u_ref, gb_ref, gbt_ref, z_ref, nw_ref, ltri_ref, ltrit_ref, o_ref, sout_ref, s_ref):
    step = pl.program_id(1)
    n = GDN_TILE
    c = GDN_CHUNK

    @pl.when(step == 0)
    def _():
        s_ref[...] = jnp.zeros_like(s_ref)

    gb = gb_ref[0]
    g1, g2, g3 = _split3(gb)
    ltri = ltri_ref[...]
    cum_cols = _dot(ltri, g1) + (_dot(ltri, g2) + _dot(ltri, g3))
    t1, t2, t3 = _split3(gbt_ref[0])
    ltrit = ltrit_ref[...]
    cum_rows = _dot(t1, ltrit) + (_dot(t2, ltrit) + _dot(t3, ltrit))

    row = lax.broadcasted_iota(jnp.int32, (n, n), 0)
    col = lax.broadcasted_iota(jnp.int32, (n, n), 1)
    same = (row >= c) == (col >= c)
    causal = same & (row >= col)
    strict = same & (row > col)
    eye = (row == col).astype(F32)
    second = lax.broadcasted_iota(jnp.int32, (n, 1), 0) >= c
    zeros_c = jnp.zeros((c, GDN_D), F32)

    heads = range(GDN_HEADS)
    lanes = lambda hd, base: slice(base + hd * GDN_D, base + (hd + 1) * GDN_D)
    q = [u_ref[0, :, lanes(hd, 0)] for hd in heads]
    k = [u_ref[0, :, lanes(hd, GDN_WIDTH)] for hd in heads]
    v = [u_ref[0, :, lanes(hd, 2 * GDN_WIDTH)] for hd in heads]
    cc = [cum_cols[:, hd:hd + 1] for hd in heads]
    beta = [gb[:, GDN_HEADS + hd:GDN_HEADS + hd + 1] for hd in heads]
    gam = [jnp.where(causal, jnp.exp(jnp.where(causal, cc[hd] - cum_rows[hd:hd + 1, :], 0.0)), 0.0) for hd in heads]
    kb = [k[hd] * beta[hd] for hd in heads]
    k16 = [k[hd].astype(BF16) for hd in heads]
    a_mat = [jnp.where(strict, _dot_nt(kb[hd].astype(BF16), k16[hd]) * gam[hd], 0.0) for hd in heads]
    qk = [(_dot_nt(q[hd].astype(BF16), k16[hd]) * gam[hd]).astype(BF16) for hd in heads]
    inv = [eye - a_mat[hd] for hd in heads]
    pw = [_dot3(a_mat[hd], a_mat[hd]) for hd in heads]
    for _ in range(4):
        both = [_dot3(jnp.concatenate([inv[hd], pw[hd]], axis=0), pw[hd]) for hd in heads]
        inv = [inv[hd] + both[hd][:n] for hd in heads]
        pw = [both[hd][n:] for hd in heads]
    inv = [inv[hd] + _dot3(inv[hd], pw[hd]) for hd in heads]
    ecum = [jnp.exp(cc[hd]) for hd in heads]
    uw = [_dot3(inv[hd], jnp.concatenate([v[hd] * beta[hd], kb[hd] * ecum[hd]], axis=1)) for hd in heads]
    uu = [uw[hd][:, :GDN_D] for hd in heads]
    ww = [uw[hd][:, GDN_D:].astype(BF16) for hd in heads]
    qg = [(q[hd] * ecum[hd]).astype(BF16) for hd in heads]
    g_last0 = [cc[hd][c - 1:c] for hd in heads]
    g_last1 = [cc[hd][n - 1:n] for hd in heads]
    kdt = [(k[hd] * jnp.exp(jnp.where(second, g_last1[hd], g_last0[hd]) - cc[hd])).T.astype(BF16)
           for hd in heads]

    s0 = [s_ref[hd] for hd in heads]
    s0b = [s0[hd].astype(BF16) for hd in heads]
    vn0 = [uu[hd][:c] - _dot(ww[hd][:c], s0b[hd]) for hd in heads]
    oi0 = [_dot(qg[hd][:c], s0b[hd]) for hd in heads]
    s1 = [s0[hd] * jnp.exp(g_last0[hd]) + _dot(kdt[hd], jnp.concatenate([vn0[hd], zeros_c], axis=0).astype(BF16))
          for hd in heads]
    s1b = [s1[hd].astype(BF16) for hd in heads]
    vn1 = [uu[hd][c:] - _dot(ww[hd][c:], s1b[hd]) for hd in heads]
    oi1 = [_dot(qg[hd][c:], s1b[hd]) for hd in heads]
    s2 = [s1[hd] * jnp.exp(g_last1[hd]) + _dot(kdt[hd], jnp.concatenate([zeros_c, vn1[hd]], axis=0).astype(BF16))
          for hd in heads]
    for hd in heads:
        s_ref[hd] = s2[hd]
        vn = jnp.concatenate([vn0[hd], vn1[hd]], axis=0).astype(BF16)
        o = jnp.concatenate([oi0[hd], oi1[hd]], axis=0) + _dot(qk[hd], vn)
        o_ref[0, :, lanes(hd, 0)] = _gated_norm(o, z_ref[0, :, lanes(hd, 0)], nw_ref[...]).astype(BF16)

    @pl.when(step == pl.num_programs(1) - 1)
    def _():
        sout_ref[0] = s_ref[...]


def _gdn_prompt(u, gb, gbt, z, nw, ltri, ltrit):
    b, t, _ = u.shape
    n = GDN_TILE
    row = lambda i, j: (i, j, 0)
    const = lambda i, j: (0, 0)
    return pl.pallas_call(
        _gdn_chunk_kernel,
        name="gdn_prompt",
        grid=(b, t // n),
        in_specs=[
            pl.BlockSpec((1, n, CONV_DIM), row),
            pl.BlockSpec((1, n, AB_COLS), row),
            pl.BlockSpec((1, 8, n), lambda i, j: (i, 0, j)),
            pl.BlockSpec((1, n, GDN_WIDTH), row),
            pl.BlockSpec((1, GDN_D), const),
            pl.BlockSpec((n, n), const),
            pl.BlockSpec((n, n), const),
        ],
        out_specs=[
            pl.BlockSpec((1, n, GDN_WIDTH), row),
            pl.BlockSpec((1, GDN_HEADS, GDN_D, GDN_D), lambda i, j: (i, 0, 0, 0)),
        ],
        out_shape=[
            jax.ShapeDtypeStruct((b, t, GDN_WIDTH), BF16),
            jax.ShapeDtypeStruct((b, GDN_HEADS, GDN_D, GDN_D), F32),
        ],
        scratch_shapes=[pltpu.VMEM((GDN_HEADS, GDN_D, GDN_D), F32)],
        compiler_params=_cparams(("parallel", "arbitrary")),
    )(u, gb, gbt, z, nw, ltri, ltrit)


def _gdn_step_kernel(s_ref, u_ref, gb_ref, z_ref, nw_ref, sout_ref, o_ref):
    row = lax.broadcasted_iota(jnp.int32, (GDN_D, GDN_D), 0)
    col = lax.broadcasted_iota(jnp.int32, (GDN_D, GDN_D), 1)
    eye = row == col

    def column(r):
        return jnp.sum(jnp.where(eye, r, 0.0), axis=1, keepdims=True)

    for hd in range(GDN_HEADS):
        q = u_ref[0, :, hd * GDN_D:(hd + 1) * GDN_D]
        k = u_ref[0, :, GDN_WIDTH + hd * GDN_D:GDN_WIDTH + (hd + 1) * GDN_D]
        v = u_ref[0, :, 2 * GDN_WIDTH + hd * GDN_D:2 * GDN_WIDTH + (hd + 1) * GDN_D]
        g = gb_ref[0, :, hd:hd + 1]
        beta = gb_ref[0, :, GDN_HEADS + hd:GDN_HEADS + hd + 1]
        kc = column(k)
        s = s_ref[0, hd] * jnp.exp(g)
        err = v - jnp.sum(kc * s, axis=0, keepdims=True)
        s = s + kc * (err * beta)
        sout_ref[0, hd] = s
        o = jnp.sum(column(q) * s, axis=0, keepdims=True)
        zg = z_ref[0, :, hd * GDN_D:(hd + 1) * GDN_D]
        o_ref[0, :, hd * GDN_D:(hd + 1) * GDN_D] = _gated_norm(o, zg, nw_ref[...]).astype(BF16)


def _gdn_sample(state, u, gb, z, nw):
    n = state.shape[0]
    row = lambda i: (i, 0, 0)
    st = lambda i: (i, 0, 0, 0)
    return pl.pallas_call(
        _gdn_step_kernel,
        name="gdn_sample",
        grid=(n,),
        in_specs=[
            pl.BlockSpec((1, GDN_HEADS, GDN_D, GDN_D), st),
            pl.BlockSpec((1, 1, CONV_DIM), row),
            pl.BlockSpec((1, 1, AB_COLS), row),
            pl.BlockSpec((1, 1, GDN_WIDTH), row),
            pl.BlockSpec((1, GDN_D), lambda i: (0, 0)),
        ],
        out_specs=[
            pl.BlockSpec((1, GDN_HEADS, GDN_D, GDN_D), st),
            pl.BlockSpec((1, 1, GDN_WIDTH), row),
        ],
        out_shape=[
            jax.ShapeDtypeStruct(state.shape, F32),
            jax.ShapeDtypeStruct((n, 1, GDN_WIDTH), BF16),
        ],
        compiler_params=_cparams(("parallel",)),
    )(state, u.reshape(n, 1, CONV_DIM), gb.reshape(n, 1, AB_COLS), z.reshape(n, 1, GDN_WIDTH), nw)


def _sb_blocks(zns, later, carries, mask):
    stays = [[jnp.minimum(zn, 0.0) - jnp.log2(1.0 + jnp.exp2(-jnp.abs(zn))) for zn in ch] for ch in zns]
    if mask is not None:
        stays = [[jnp.where(mask, s, 0.0) for s in ch] for ch in stays]
    scans = [[_dot(s.astype(BF16), later) for s in ch] for ch in stays]
    sums = [[jnp.sum(s, axis=1, keepdims=True) for s in ch] for ch in stays]
    ws, out_carries = [], []
    for c, carry in enumerate(carries):
        ws.append([])
        for zn, stay, scan, rs in zip(zns[c], stays[c], scans[c], sums[c]):
            w = jnp.exp2((stay - zn) + (scan + carry))
            if mask is not None:
                w = jnp.where(mask, w, 0.0)
            ws[c].append(w.astype(BF16))
            carry = carry + rs
        out_carries.append(carry)
    return ws, out_carries


def _sb_prompt_kernel(bias_ref, q_ref, kt_ref, v_ref, later_ref, o_ref, acc_ref, car_ref):
    hp = pl.program_id(1)
    i = pl.program_id(2)
    nq = q_ref.shape[2]
    row = lax.broadcasted_iota(jnp.int32, (nq, nq), 0)
    col = lax.broadcasted_iota(jnp.int32, (nq, nq), 1)
    diag_mask = col < row
    heads = range(2)

    def block(blk, carries, mask):
        span = pl.ds(pl.multiple_of(blk * nq, nq), nq)
        zns = [[_dot(q_ref[0, hh], kt_ref[0, hh, :, span]) + bias_ref[2 * hp + hh]] for hh in heads]
        ws, carries = _sb_blocks(zns, later_ref[...], carries, mask)
        return [_dot(ws[hh][0], v_ref[0, hh, span, :]) for hh in heads], carries

    outs, carries = block(i, [jnp.zeros((nq, 1), F32)] * 2, diag_mask)
    for hh in heads:
        acc_ref[hh] = outs[hh]
        car_ref[hh] = carries[hh]

    def body(jj, _):
        outs, carries = block(i - 1 - jj, [car_ref[hh] for hh in heads], None)
        for hh in heads:
            acc_ref[hh] += outs[hh]
            car_ref[hh] = carries[hh]
        return 0

    lax.fori_loop(0, i, body, 0)
    o_ref[0] = jnp.concatenate([acc_ref[hh] for hh in heads], axis=1).astype(BF16)


def _sb_prompt(q, kt, v, bias, later, *, tq):
    b, nh, t, dh = q.shape
    kt = kt.reshape(b, nh, dh, t)
    return pl.pallas_call(
        _sb_prompt_kernel,
        name="sb_prompt",
        grid_spec=pltpu.PrefetchScalarGridSpec(
            num_scalar_prefetch=0,
            grid=(b, nh // 2, t // tq),
            in_specs=[
                pl.BlockSpec(memory_space=pltpu.SMEM),
                pl.BlockSpec((1, 2, tq, dh), lambda i, h, j: (i, h, j, 0)),
                pl.BlockSpec((1, 2, dh, t), lambda i, h, j: (i, h, 0, 0)),
                pl.BlockSpec((1, 2, t, dh), lambda i, h, j: (i, h, 0, 0)),
                pl.BlockSpec((tq, tq), lambda i, h, j: (0, 0)),
            ],
            out_specs=pl.BlockSpec((1, tq, 2 * dh), lambda i, h, j: (i, j, h)),
            scratch_shapes=[pltpu.VMEM((2, tq, dh), F32), pltpu.VMEM((2, tq, 1), F32)],
        ),
        out_shape=jax.ShapeDtypeStruct((b, t, nh * dh), BF16),
        compiler_params=_cparams(("parallel", "parallel", "arbitrary")),
    )(bias, q, kt, v, later)


def _sb_decode_kernel(pt_ref, q_ref, bias_ref, later_ref, *refs, n_pages):
    k_refs = refs[:n_pages]
    v_refs = refs[n_pages:2 * n_pages]
    o_ref = refs[2 * n_pages]
    head_of_lane = lax.broadcasted_iota(jnp.int32, (SB_HEADS, SB_WIDTH), 1) // SB_DH
    own = head_of_lane == lax.broadcasted_iota(jnp.int32, (SB_HEADS, SB_WIDTH), 0)
    qbd = jnp.where(own, q_ref[0], 0.0).astype(BF16)
    order = range(n_pages - 1, -1, -1)
    zs = [_dot(qbd, k_refs[p][0].astype(BF16)) + bias_ref[...] for p in order]
    ws, _ = _sb_blocks([zs], later_ref[...], [jnp.zeros((SB_HEADS, 1), F32)], None)
    acc = _dot_nt(ws[0][0], v_refs[order[0]][0].astype(BF16))
    for w, p in zip(ws[0][1:], order[1:]):
        acc = acc + _dot_nt(w, v_refs[p][0].astype(BF16))
    o_ref[0] = jnp.sum(jnp.where(own, acc, 0.0), axis=0, keepdims=True).astype(BF16)


def _sb_decode(q, cache_k, cache_v, page_table, bias_b, later, *, layer):
    n, n_pages = page_table.shape
    depth, n_pool, page, nh, dh = cache_k.shape
    ck = jnp.transpose(cache_k, (0, 1, 3, 4, 2)).reshape(depth * n_pool, nh * dh, page)
    cv = jnp.transpose(cache_v, (0, 1, 3, 4, 2)).reshape(depth * n_pool, nh * dh, page)

    def page_spec(p):
        return pl.BlockSpec((1, nh * dh, page), lambda i, pt: (layer * n_pool + pt[i * n_pages + p], 0, 0))

    row = lambda i, pt: (i, 0, 0)
    return pl.pallas_call(
        functools.partial(_sb_decode_kernel, n_pages=n_pages),
        name="sb_decode",
        grid_spec=pltpu.PrefetchScalarGridSpec(
            num_scalar_prefetch=1,
            grid=(n,),
            in_specs=[
                pl.BlockSpec((1, 1, nh * dh), row),
                pl.BlockSpec((nh, 128), lambda i, pt: (0, 0)),
                pl.BlockSpec((page, page), lambda i, pt: (0, 0)),
            ] + [page_spec(p) for p in range(n_pages)] * 2,
            out_specs=pl.BlockSpec((1, 1, nh * dh), row),
        ),
        out_shape=jax.ShapeDtypeStruct((n, 1, nh * dh), BF16),
        compiler_params=_cparams(("parallel",)),
    )(page_table.reshape(-1), q.reshape(n, 1, nh * dh), bias_b, later, *([ck] * n_pages), *([cv] * n_pages))


def _mlp_kernel(x_ref, oa_ref, ob_ref, wo_ref, nw_ref, wup_ref, wdn_ref, y_ref, x1_ref, xn_ref, acc_ref):
    j = pl.program_id(1)

    @pl.when(j == 0)
    def _():
        x1 = x_ref[...] + (_dot(oa_ref[...], wo_ref[:GDN_WIDTH]) + _dot(ob_ref[...], wo_ref[GDN_WIDTH:]))
        x1_ref[...] = x1
        ms = jnp.mean(x1 * x1, axis=-1, keepdims=True)
        xn_ref[...] = (x1 * lax.rsqrt(ms + EPS) * nw_ref[...]).astype(BF16)
        acc_ref[...] = jnp.zeros_like(acc_ref)

    h = jnp.maximum(_dot(xn_ref[...], wup_ref[...]), 0.0)
    acc_ref[...] += _dot((h * h).astype(BF16), wdn_ref[...])

    @pl.when(j == pl.num_programs(1) - 1)
    def _():
        y_ref[...] = x1_ref[...] + acc_ref[...]


def _mix_mlp(x, oa, ob, wo, nw, wup, wdn, *, tm, tf):
    m, d = x.shape
    ff = wup.shape[1]
    row = lambda i, j: (i, 0)
    const = lambda i, j: (0, 0)
    return pl.pallas_call(
        _mlp_kernel,
        name="mix_mlp",
        grid=(m // tm, ff // tf),
        in_specs=[
            pl.BlockSpec((tm, d), row),
            pl.BlockSpec((tm, GDN_WIDTH), row),
            pl.BlockSpec((tm, SB_WIDTH), row),
            pl.BlockSpec((d, d), const),
            pl.BlockSpec((1, d), const),
            pl.BlockSpec((d, tf), lambda i, j: (0, j)),
            pl.BlockSpec((tf, d), lambda i, j: (j, 0)),
        ],
        out_specs=pl.BlockSpec((tm, d), row),
        out_shape=jax.ShapeDtypeStruct((m, d), F32),
        scratch_shapes=[pltpu.VMEM((tm, d), F32), pltpu.VMEM((tm, d), BF16), pltpu.VMEM((tm, d), F32)],
        compiler_params=_cparams(("parallel", "arbitrary")),
    )(x, oa, ob, wo, nw, wup, wdn)


def _tri_consts(n_chunk_tile, chunk, tq, page):
    r = jnp.arange(n_chunk_tile)
    same = (r[:, None] // chunk) == (r[None, :] // chunk)
    ltri = (same & (r[None, :] <= r[:, None])).astype(BF16)

    def later(nk):
        k = jnp.arange(nk)
        return (k[:, None] > k[None, :]).astype(BF16)

    return ltri, ltri.T, later(tq), later(page)


def kernel(x_prompt, x_sample, cache_k, cache_v, page_table, state_gdn, state_conv, meta_tokens, norm1_w, w_in,
           conv_w, a_log, dt_bias, gdn_norm_w, q_norm_w, k_norm_w, sb_bias, w_o, norm2_w, w_up, w_down):
    depth, d_model = norm1_w.shape
    bp, seq, _ = x_prompt.shape
    bs = x_sample.shape[0]
    page = cache_k.shape[2]
    t_real = N_META + seq
    tp = -(-t_real // ROW_TILE) * ROW_TILE

    c0 = CONV_DIM
    c1 = c0 + GDN_WIDTH
    c2 = c1 + GDN_HEADS
    c3 = c2 + GDN_HEADS
    w_in_r = jnp.concatenate(
        [w_in[..., :c1], w_in[..., c3:], w_in[..., c1:c3],
         jnp.zeros((depth, d_model, AB_COLS - 2 * GDN_HEADS), w_in.dtype)], axis=-1).astype(BF16)
    w_o16, w_up16, w_dn16 = w_o.astype(BF16), w_up.astype(BF16), w_down.astype(BF16)
    pad_lanes = lambda a: jnp.pad(a, ((0, 0), (0, AB_COLS - a.shape[1])))[:, None, :]
    pa, pb = pad_lanes(a_log), pad_lanes(dt_bias)
    qnw = jnp.tile(q_norm_w, (1, SB_HEADS))[:, None, :]
    knw = jnp.tile(k_norm_w, (1, SB_HEADS))[:, None, :]
    lane_head = jnp.arange(SB_WIDTH) // SB_DH
    gseg = (lane_head[:, None] == lane_head[None, :]).astype(BF16)
    ltri, ltrit, later_q, later_p = _tri_consts(GDN_TILE, GDN_CHUNK, ROW_TILE, page)
    sb_bias2 = -sb_bias * LOG2E
    bias_b = jnp.broadcast_to(sb_bias2[:, :, None], (depth, SB_HEADS, 128))

    meta = jnp.broadcast_to(meta_tokens[None], (bp, N_META, d_model))
    xp = jnp.concatenate([meta, x_prompt, jnp.zeros((bp, tp - t_real, d_model), x_prompt.dtype)], axis=1)
    xs = x_sample.reshape(1, bs, d_model)
    st_conv = state_conv.reshape(depth, bs, (CONV_TAPS - 1) * CONV_DIM)

    kp_l, vp_l, sp_l, cp_l = [], [], [], []
    ks_l, vs_l, ss_l, cs_l = [], [], [], []
    for l in range(depth):
        n1, n2 = norm1_w[l][None], norm2_w[l][None]
        gnw = gdn_norm_w[l][None]
        conv_in, z, ab, q, kt, v, knew, vnew = _inproj(xp, n1, w_in_r[l], gseg, qnw[l], knw[l], prompt=True,
                                                      tm=ROW_TILE)
        u, gb, gbt = _conv_prompt(conv_in, conv_w[l], ab, pa[l], pb[l], t_real=t_real, tm=ROW_TILE)
        o_a, s_fin = _gdn_prompt(u, gb, gbt, z, gnw, ltri, ltrit)
        o_b = _sb_prompt(q, kt, v, sb_bias2[l], later_q, tq=ROW_TILE)
        xp = _mix_mlp(xp.reshape(bp * tp, d_model), o_a.reshape(bp * tp, GDN_WIDTH), o_b.reshape(bp * tp, SB_WIDTH),
                      w_o16[l], n2, w_up16[l], w_dn16[l], tm=512, tf=1024).reshape(bp, tp, d_model)
        kp_l.append(knew[:, :t_real].reshape(bp, t_real, SB_HEADS, SB_DH))
        vp_l.append(vnew[:, :t_real].reshape(bp, t_real, SB_HEADS, SB_DH))
        sp_l.append(s_fin)
        cp_l.append(conv_in[:, t_real - (CONV_TAPS - 1):t_real])

        conv_in, z, ab, q, knew, vnew = _inproj(xs, n1, w_in_r[l], gseg, qnw[l], knw[l], prompt=False, tm=bs)
        u, gb, ns = _conv_sample(conv_in[0], st_conv[l], conv_w[l], ab[0], pa[l], pb[l])
        s_new, o_a = _gdn_sample(state_gdn[l], u, gb, z[0], gnw)
        o_b = _sb_decode(q[0], cache_k, cache_v, page_table, bias_b[l], later_p, layer=l)
        xs = _mix_mlp(xs[0], o_a.reshape(bs, GDN_WIDTH), o_b.reshape(bs, SB_WIDTH), w_o16[l], n2, w_up16[l],
                      w_dn16[l], tm=bs, tf=1024)[None]
        ks_l.append(knew.reshape(bs, 1, SB_HEADS, SB_DH))
        vs_l.append(vnew.reshape(bs, 1, SB_HEADS, SB_DH))
        ss_l.append(s_new)
        cs_l.append(ns.reshape(bs, CONV_TAPS - 1, CONV_DIM))

    y_prompt = xp[:, N_META:t_real]
    return (y_prompt, xs.reshape(bs, 1, d_model),
            jnp.stack(kp_l), jnp.stack(vp_l), jnp.stack(sp_l), jnp.stack(cp_l),
            jnp.stack(ks_l), jnp.stack(vs_l), jnp.stack(ss_l), jnp.stack(cs_l))
```

```python
import functools

import jax
import jax.numpy as jnp
from jax import lax
from jax.experimental import pallas as pl
from jax.experimental.pallas import tpu as pltpu

F32 = jnp.float32
BF16 = jnp.bfloat16
EPS = 1e-6
LOG2E = 1.4426950408889634

N_META = 16
GDN_HEADS = 4
GDN_D = 128
GDN_WIDTH = GDN_HEADS * GDN_D
CONV_DIM = 3 * GDN_WIDTH
CONV_TAPS = 4
GDN_CHUNK = 64
SB_HEADS = 8
SB_DH = 64
SB_WIDTH = SB_HEADS * SB_DH
AB_COLS = 128
C_CONV = 0
C_Z = C_CONV + CONV_DIM
C_SQ = C_Z + GDN_WIDTH
C_SK = C_SQ + SB_WIDTH
C_SV = C_SK + SB_WIDTH
C_AB = C_SV + SB_WIDTH
IN_COLS_PAD = C_AB + AB_COLS

ROW_TILE = 256
GDN_TILE = 2 * GDN_CHUNK
VMEM_LIMIT = 56 * 1024 * 1024


def _cparams(sem):
    return pltpu.CompilerParams(dimension_semantics=sem, vmem_limit_bytes=VMEM_LIMIT)


def _dot(a, b):
    return jnp.dot(a, b, preferred_element_type=F32)


def _dot_nt(a, b):
    return lax.dot_general(a, b, (((1,), (1,)), ((), ())), preferred_element_type=F32)


def _split2(x):
    hi = x.astype(BF16)
    lo = (x - hi.astype(F32)).astype(BF16)
    return hi, lo


def _split3(x):
    hi = x.astype(BF16)
    r = x - hi.astype(F32)
    mid = r.astype(BF16)
    lo = (r - mid.astype(F32)).astype(BF16)
    return hi, mid, lo


def _dot3(a, b):
    ah, al = _split2(a)
    bh, bl = _split2(b)
    return _dot(ah, bh) + (_dot(ah, bl) + _dot(al, bh))


def _sigmoid(x):
    return 1.0 / (1.0 + jnp.exp(-x))


def _softplus(x):
    return jnp.maximum(x, 0.0) + jnp.log(1.0 + jnp.exp(-jnp.abs(x)))


def _inproj_kernel(x_ref, nw_ref, w_ref, gseg_ref, qnw_ref, knw_ref, *outs, prompt):
    x = x_ref[0]
    ms = jnp.mean(x * x, axis=-1, keepdims=True)
    h = (x * lax.rsqrt(ms + EPS) * nw_ref[...]).astype(BF16)

    def proj(lo, hi):
        return _dot(h, w_ref[:, lo:hi])

    def headnorm(s, wrow):
        hi, lo = _split2(s * s)
        ss = _dot(hi, gseg_ref[...]) + _dot(lo, gseg_ref[...])
        return s * lax.rsqrt(ss * (1.0 / SB_DH) + EPS) * wrow

    qn = headnorm(proj(C_SQ, C_SK), qnw_ref[...]) * (-(SB_DH ** -0.5) * LOG2E)
    kn = headnorm(proj(C_SK, C_SV), knw_ref[...])
    sv = proj(C_SV, C_AB)
    if prompt:
        conv_ref, z_ref, ab_ref, q_ref, kt_ref, v_ref, knew_ref, vnew_ref = outs
        for hd in range(SB_HEADS):
            q_ref[0, hd] = qn[:, hd * SB_DH:(hd + 1) * SB_DH].astype(BF16)
            v_ref[0, hd] = sv[:, hd * SB_DH:(hd + 1) * SB_DH].astype(BF16)
        knt = kn.T
        kt_ref[0] = knt.astype(BF16)
        knew_ref[0] = knt
        vnew_ref[0] = sv.T
    else:
        conv_ref, z_ref, ab_ref, q_ref, knew_ref, vnew_ref = outs
        q_ref[0] = qn
        knew_ref[0] = kn
        vnew_ref[0] = sv
    conv_ref[0] = proj(C_CONV, C_Z)
    z_ref[0] = proj(C_Z, C_SQ)
    ab_ref[0] = proj(C_AB, IN_COLS_PAD)


def _inproj(x, nw, w, gseg, qnw, knw, *, prompt, tm, t_out=None):
    b, t, d = x.shape
    grid = (b, t // tm)
    row = lambda i, j: (i, j, 0)
    const = lambda i, j: (0, 0)
    in_specs = [
        pl.BlockSpec((1, tm, d), row),
        pl.BlockSpec((1, d), const),
        pl.BlockSpec((d, IN_COLS_PAD), const),
        pl.BlockSpec((SB_WIDTH, SB_WIDTH), const),
        pl.BlockSpec((1, SB_WIDTH), const),
        pl.BlockSpec((1, SB_WIDTH), const),
    ]
    f32_rows = lambda n: (jax.ShapeDtypeStruct((b, t, n), F32), pl.BlockSpec((1, tm, n), row))
    outs = [f32_rows(CONV_DIM), f32_rows(GDN_WIDTH), f32_rows(AB_COLS)]
    if prompt:
        hm = lambda i, j: (i, 0, j, 0)
        outs += [
            (jax.ShapeDtypeStruct((b, SB_HEADS, t, SB_DH), BF16), pl.BlockSpec((1, SB_HEADS, tm, SB_DH), hm)),
            (jax.ShapeDtypeStruct((b, SB_WIDTH, t), BF16), pl.BlockSpec((1, SB_WIDTH, tm), lambda i, j: (i, 0, j))),
            (jax.ShapeDtypeStruct((b, SB_HEADS, t, SB_DH), BF16), pl.BlockSpec((1, SB_HEADS, tm, SB_DH), hm)),
        ]
        cols = lambda i, j: (i, 0, j)
        outs += [(jax.ShapeDtypeStruct((b, SB_WIDTH, t_out), F32), pl.BlockSpec((1, SB_WIDTH, tm), cols))] * 2
    else:
        outs += [f32_rows(SB_WIDTH)] * 3
    return pl.pallas_call(
        functools.partial(_inproj_kernel, prompt=prompt),
        name="inproj_prompt" if prompt else "inproj_sample",
        grid=grid,
        in_specs=in_specs,
        out_specs=[o[1] for o in outs],
        out_shape=[o[0] for o in outs],
        compiler_params=_cparams(("parallel", "parallel")),
    )(x, nw, w, gseg, qnw, knw)


def _gdn_inputs(conv_out, ab, pa, pb, valid):
    u = conv_out * _sigmoid(conv_out)
    parts = []
    for hd in range(2 * GDN_HEADS):
        sl = u[:, hd * GDN_D:(hd + 1) * GDN_D]
        nrm = sl * lax.rsqrt(jnp.sum(sl * sl, axis=-1, keepdims=True) + EPS)
        parts.append(nrm * (GDN_D ** -0.5) if hd < GDN_HEADS else nrm)
    parts.append(u[:, 2 * GDN_WIDTH:])
    g = -jnp.exp(pa) * _softplus(ab + pb)
    beta = _sigmoid(ab)
    lane = lax.broadcasted_iota(jnp.int32, ab.shape, 1)
    gb = jnp.where(lane < GDN_HEADS, g, jnp.where(lane < 2 * GDN_HEADS, beta, 0.0))
    if valid is not None:
        gb = jnp.where(valid, gb, 0.0)
    return jnp.concatenate(parts, axis=1), gb


def _conv_prompt_kernel(x_ref, prev_ref, cw_ref, ab_ref, pa_ref, pb_ref, u_ref, gb_ref, gbt_ref, xs_ref, *, t_real):
    i = pl.program_id(1)
    tm = x_ref.shape[1]
    xs_ref[0:8] = jnp.where(i > 0, prev_ref[0], 0.0)
    xs_ref[8:] = x_ref[0]
    out = xs_ref[pl.ds(5, tm)] * cw_ref[0:1]
    for tap in range(1, CONV_TAPS):
        out = out + xs_ref[pl.ds(5 + tap, tm)] * cw_ref[tap:tap + 1]
    row = i * tm + lax.broadcasted_iota(jnp.int32, (tm, AB_COLS), 0)
    u, gb = _gdn_inputs(out, ab_ref[0], pa_ref[...], pb_ref[...], row < t_real)
    u_ref[0] = u
    gb_ref[0] = gb
    gbt_ref[0] = gb.T[0:8]


def _conv_prompt(conv_in, cw, ab, pa, pb, *, t_real, tm):
    b, t, c = conv_in.shape
    row = lambda i, j: (i, j, 0)
    const = lambda i, j: (0, 0)
    return pl.pallas_call(
        functools.partial(_conv_prompt_kernel, t_real=t_real),
        name="conv_prompt",
        grid=(b, t // tm),
        in_specs=[
            pl.BlockSpec((1, tm, c), row),
            pl.BlockSpec((1, 8, c), lambda i, j: (i, jnp.maximum(j * (tm // 8) - 1, 0), 0)),
            pl.BlockSpec((CONV_TAPS, c), const),
            pl.BlockSpec((1, tm, AB_COLS), row),
            pl.BlockSpec((1, AB_COLS), const),
            pl.BlockSpec((1, AB_COLS), const),
        ],
        out_specs=[
            pl.BlockSpec((1, tm, c), row),
            pl.BlockSpec((1, tm, AB_COLS), row),
            pl.BlockSpec((1, 8, tm), lambda i, j: (i, 0, j)),
        ],
        out_shape=[
            jax.ShapeDtypeStruct((b, t, c), F32),
            jax.ShapeDtypeStruct((b, t, AB_COLS), F32),
            jax.ShapeDtypeStruct((b, 8, t), F32),
        ],
        scratch_shapes=[pltpu.VMEM((tm + 8, c), F32)],
        compiler_params=_cparams(("parallel", "parallel")),
    )(conv_in, conv_in, cw, ab, pa, pb)


def _conv_sample_kernel(x_ref, st_ref, cw_ref, ab_ref, pa_ref, pb_ref, u_ref, gb_ref, ns_ref):
    c = x_ref.shape[1]
    taps = [st_ref[:, k * c:(k + 1) * c] for k in range(CONV_TAPS - 1)] + [x_ref[...]]
    out = taps[0] * cw_ref[0:1]
    for tap in range(1, CONV_TAPS):
        out = out + taps[tap] * cw_ref[tap:tap + 1]
    u, gb = _gdn_inputs(out, ab_ref[...], pa_ref[...], pb_ref[...], None)
    u_ref[...] = u
    gb_ref[...] = gb
    for k in range(CONV_TAPS - 1):
        ns_ref[:, k * c:(k + 1) * c] = taps[k + 1]


def _conv_sample(conv_in, state, cw, ab, pa, pb):
    n, c = conv_in.shape
    return pl.pallas_call(
        _conv_sample_kernel,
        name="conv_sample",
        out_shape=[
            jax.ShapeDtypeStruct((n, c), F32),
            jax.ShapeDtypeStruct((n, AB_COLS), F32),
            jax.ShapeDtypeStruct((n, (CONV_TAPS - 1) * c), F32),
        ],
        compiler_params=pltpu.CompilerParams(vmem_limit_bytes=VMEM_LIMIT),
    )(conv_in, state, cw, ab, pa, pb)


def _gated_norm(o, z, nw):
    return o * lax.rsqrt(jnp.mean(o * o, axis=-1, keepdims=True) + EPS) * nw * (z * _sigmoid(z))


def _gdn_chunk_kernel(u_ref, gb_ref, gbt_ref, z_ref, nw_ref, ltri_ref, ltrit_ref, o_ref, sout_ref, s_ref):
    step = pl.program_id(1)
    n = GDN_TILE
    c = GDN_CHUNK

    @pl.when(step == 0)
    def _():
        s_ref[...] = jnp.zeros_like(s_ref)

    gb = gb_ref[0]
    g1, g2, g3 = _split3(gb)
    ltri = ltri_ref[...]
    cum_cols = _dot(ltri, g1) + (_dot(ltri, g2) + _dot(ltri, g3))
    t1, t2, t3 = _split3(gbt_ref[0])
    ltrit = ltrit_ref[...]
    cum_rows = _dot(t1, ltrit) + (_dot(t2, ltrit) + _dot(t3, ltrit))

    row = lax.broadcasted_iota(jnp.int32, (n, n), 0)
    col = lax.broadcasted_iota(jnp.int32, (n, n), 1)
    same = (row >= c) == (col >= c)
    causal = same & (row >= col)
    strict = same & (row > col)
    eye = (row == col).astype(F32)
    second = lax.broadcasted_iota(jnp.int32, (n, 1), 0) >= c
    zeros_c = jnp.zeros((c, GDN_D), F32)

    heads = range(GDN_HEADS)
    lanes = lambda hd, base: slice(base + hd * GDN_D, base + (hd + 1) * GDN_D)
    q = [u_ref[0, :, lanes(hd, 0)] for hd in heads]
    k = [u_ref[0, :, lanes(hd, GDN_WIDTH)] for hd in heads]
    v = [u_ref[0, :, lanes(hd, 2 * GDN_WIDTH)] for hd in heads]
    cc = [cum_cols[:, hd:hd + 1] for hd in heads]
    beta = [gb[:, GDN_HEADS + hd:GDN_HEADS + hd + 1] for hd in heads]
    gam = [jnp.where(causal, jnp.exp(jnp.where(causal, cc[hd] - cum_rows[hd:hd + 1, :], 0.0)), 0.0) for hd in heads]
    kb = [k[hd] * beta[hd] for hd in heads]
    k16 = [k[hd].astype(BF16) for hd in heads]
    a_mat = [jnp.where(strict, _dot_nt(kb[hd].astype(BF16), k16[hd]) * gam[hd], 0.0) for hd in heads]
    qk = [(_dot_nt(q[hd].astype(BF16), k16[hd]) * gam[hd]).astype(BF16) for hd in heads]
    inv = [eye - a_mat[hd] for hd in heads]
    pw = [_dot3(a_mat[hd], a_mat[hd]) for hd in heads]
    for _ in range(4):
        both = [_dot3(jnp.concatenate([inv[hd], pw[hd]], axis=0), pw[hd]) for hd in heads]
        inv = [inv[hd] + both[hd][:n] for hd in heads]
        pw = [both[hd][n:] for hd in heads]
    inv = [inv[hd] + _dot3(inv[hd], pw[hd]) for hd in heads]
    ecum = [jnp.exp(cc[hd]) for hd in heads]
    uw = [_dot3(inv[hd], jnp.concatenate([v[hd] * beta[hd], kb[hd] * ecum[hd]], axis=1)) for hd in heads]
    uu = [uw[hd][:, :GDN_D] for hd in heads]
    ww = [uw[hd][:, GDN_D:].astype(BF16) for hd in heads]
    qg = [(q[hd] * ecum[hd]).astype(BF16) for hd in heads]
    g_last0 = [cc[hd][c - 1:c] for hd in heads]
    g_last1 = [cc[hd][n - 1:n] for hd in heads]
    kdt = [(k[hd] * jnp.exp(jnp.where(second, g_last1[hd], g_last0[hd]) - cc[hd])).T.astype(BF16)
           for hd in heads]

    s0 = [s_ref[hd] for hd in heads]
    s0b = [s0[hd].astype(BF16) for hd in heads]
    vn0 = [uu[hd][:c] - _dot(ww[hd][:c], s0b[hd]) for hd in heads]
    oi0 = [_dot(qg[hd][:c], s0b[hd]) for hd in heads]
    s1 = [s0[hd] * jnp.exp(g_last0[hd]) + _dot(kdt[hd], jnp.concatenate([vn0[hd], zeros_c], axis=0).astype(BF16))
          for hd in heads]
    s1b = [s1[hd].astype(BF16) for hd in heads]
    vn1 = [uu[hd][c:] - _dot(ww[hd][c:], s1b[hd]) for hd in heads]
    oi1 = [_dot(qg[hd][c:], s1b[hd]) for hd in heads]
    s2 = [s1[hd] * jnp.exp(g_last1[hd]) + _dot(kdt[hd], jnp.concatenate([zeros_c, vn1[hd]], axis=0).astype(BF16))
          for hd in heads]
    for hd in heads:
        s_ref[hd] = s2[hd]
        vn = jnp.concatenate([vn0[hd], vn1[hd]], axis=0).astype(BF16)
        o = jnp.concatenate([oi0[hd], oi1[hd]], axis=0) + _dot(qk[hd], vn)
        o_ref[0, :, lanes(hd, 0)] = _gated_norm(o, z_ref[0, :, lanes(hd, 0)], nw_ref[...]).astype(BF16)

    @pl.when(step == pl.num_programs(1) - 1)
    def _():
        sout_ref[0] = s_ref[...]


def _gdn_prompt(u, gb, gbt, z, nw, ltri, ltrit):
    b, t, _ = u.shape
    n = GDN_TILE
    row = lambda i, j: (i, j, 0)
    const = lambda i, j: (0, 0)
    return pl.pallas_call(
        _gdn_chunk_kernel,
        name="gdn_prompt",
        grid=(b, t // n),
        in_specs=[
            pl.BlockSpec((1, n, CONV_DIM), row),
            pl.BlockSpec((1, n, AB_COLS), row),
            pl.BlockSpec((1, 8, n), lambda i, j: (i, 0, j)),
            pl.BlockSpec((1, n, GDN_WIDTH), row),
            pl.BlockSpec((1, GDN_D), const),
            pl.BlockSpec((n, n), const),
            pl.BlockSpec((n, n), const),
        ],
        out_specs=[
            pl.BlockSpec((1, n, GDN_WIDTH), row),
            pl.BlockSpec((1, GDN_HEADS, GDN_D, GDN_D), lambda i, j: (i, 0, 0, 0)),
        ],
        out_shape=[
            jax.ShapeDtypeStruct((b, t, GDN_WIDTH), BF16),
            jax.ShapeDtypeStruct((b, GDN_HEADS, GDN_D, GDN_D), F32),
        ],
        scratch_shapes=[pltpu.VMEM((GDN_HEADS, GDN_D, GDN_D), F32)],
        compiler_params=_cparams(("parallel", "arbitrary")),
    )(u, gb, gbt, z, nw, ltri, ltrit)


def _gdn_step_kernel(s_ref, u_ref, gb_ref, z_ref, nw_ref, sout_ref, o_ref):
    row = lax.broadcasted_iota(jnp.int32, (GDN_D, GDN_D), 0)
    col = lax.broadcasted_iota(jnp.int32, (GDN_D, GDN_D), 1)
    eye = row == col

    def column(r):
        return jnp.sum(jnp.where(eye, r, 0.0), axis=1, keepdims=True)

    for hd in range(GDN_HEADS):
        q = u_ref[0, :, hd * GDN_D:(hd + 1) * GDN_D]
        k = u_ref[0, :, GDN_WIDTH + hd * GDN_D:GDN_WIDTH + (hd + 1) * GDN_D]
        v = u_ref[0, :, 2 * GDN_WIDTH + hd * GDN_D:2 * GDN_WIDTH + (hd + 1) * GDN_D]
        g = gb_ref[0, :, hd:hd + 1]
        beta = gb_ref[0, :, GDN_HEADS + hd:GDN_HEADS + hd + 1]
        kc = column(k)
        s = s_ref[0, hd] * jnp.exp(g)
        err = v - jnp.sum(kc * s, axis=0, keepdims=True)
        s = s + kc * (err * beta)
        sout_ref[0, hd] = s
        o = jnp.sum(column(q) * s, axis=0, keepdims=True)
        zg = z_ref[0, :, hd * GDN_D:(hd + 1) * GDN_D]
        o_ref[0, :, hd * GDN_D:(hd + 1) * GDN_D] = _gated_norm(o, zg, nw_ref[...]).astype(BF16)


def _gdn_sample(state, u, gb, z, nw):
    n = state.shape[0]
    row = lambda i: (i, 0, 0)
    st = lambda i: (i, 0, 0, 0)
    return pl.pallas_call(
        _gdn_step_kernel,
        name="gdn_sample",
        grid=(n,),
        in_specs=[
            pl.BlockSpec((1, GDN_HEADS, GDN_D, GDN_D), st),
            pl.BlockSpec((1, 1, CONV_DIM), row),
            pl.BlockSpec((1, 1, AB_COLS), row),
            pl.BlockSpec((1, 1, GDN_WIDTH), row),
            pl.BlockSpec((1, GDN_D), lambda i: (0, 0)),
        ],
        out_specs=[
            pl.BlockSpec((1, GDN_HEADS, GDN_D, GDN_D), st),
            pl.BlockSpec((1, 1, GDN_WIDTH), row),
        ],
        out_shape=[
            jax.ShapeDtypeStruct(state.shape, F32),
            jax.ShapeDtypeStruct((n, 1, GDN_WIDTH), BF16),
        ],
        compiler_params=_cparams(("parallel",)),
    )(state, u.reshape(n, 1, CONV_DIM), gb.reshape(n, 1, AB_COLS), z.reshape(n, 1, GDN_WIDTH), nw)


def _neg_abs(x):
    bits = lax.bitcast_convert_type(x, jnp.uint32) | jnp.uint32(0x80000000)
    return lax.bitcast_convert_type(bits, F32)


def _sb_blocks(zns, later, carries, mask):
    stays = [[jnp.minimum(zn, 0.0) - jnp.log2(1.0 + jnp.exp2(_neg_abs(zn))) for zn in ch] for ch in zns]
    if mask is not None:
        stays = [[jnp.where(mask, s, 0.0) for s in ch] for ch in stays]
    scans = [[_dot(s.astype(BF16), later) for s in ch] for ch in stays]
    sums = [[jnp.sum(s, axis=1, keepdims=True) for s in ch] for ch in stays]
    ws, out_carries = [], []
    for c, carry in enumerate(carries):
        ws.append([])
        for zn, stay, scan, rs in zip(zns[c], stays[c], scans[c], sums[c]):
            w = jnp.exp2((stay - zn) + (scan + carry))
            if mask is not None:
                w = jnp.where(mask, w, 0.0)
            ws[c].append(w.astype(BF16))
            carry = carry + rs
        out_carries.append(carry)
    return ws, out_carries


def _sb_prompt_kernel(bias_ref, q_ref, kt_ref, v_ref, later_ref, o_ref, zn_ref, w_ref, acc_ref, car_ref):
    hp = pl.program_id(1)
    i = pl.program_id(2)
    nq = q_ref.shape[2]
    row = lax.broadcasted_iota(jnp.int32, (nq, nq), 0)
    col = lax.broadcasted_iota(jnp.int32, (nq, nq), 1)
    diag_mask = col < row
    heads = range(2)

    def span(blk):
        return pl.ds(pl.multiple_of(blk * nq, nq), nq)

    def logits(blk, slot):
        for hh in heads:
            zn_ref[slot, hh] = _dot(q_ref[0, hh], kt_ref[0, hh, :, span(blk)]) + bias_ref[2 * hp + hh]

    def weights(slot, mask):
        ws, carries = _sb_blocks([[zn_ref[slot, hh]] for hh in heads], later_ref[...],
                                 [car_ref[hh] for hh in heads], mask)
        for hh in heads:
            w_ref[hh] = ws[hh][0]
            car_ref[hh] = carries[hh]

    def attend(blk):
        for hh in heads:
            acc_ref[hh] += _dot(w_ref[hh], v_ref[0, hh, span(blk), :])

    def step(s, slot):
        attend(i - s + 1)
        logits(jnp.maximum(i - s - 1, 0), 1 - slot)
        weights(slot, None)

    acc_ref[...] = jnp.zeros_like(acc_ref)
    car_ref[...] = jnp.zeros_like(car_ref)
    logits(i, 0)
    logits(jnp.maximum(i - 1, 0), 1)
    weights(0, diag_mask)

    def body(p, _):
        step(2 * p + 1, 1)
        step(2 * p + 2, 0)
        return 0

    lax.fori_loop(0, i // 2, body, 0)

    @pl.when(i % 2 == 1)
    def _():
        step(i, 1)

    attend(0)
    o_ref[0] = jnp.concatenate([acc_ref[hh] for hh in heads], axis=1).astype(BF16)


def _sb_prompt(q, kt, v, bias, later, *, tq):
    b, nh, t, dh = q.shape
    kt = kt.reshape(b, nh, dh, t)
    return pl.pallas_call(
        _sb_prompt_kernel,
        name="sb_prompt",
        grid_spec=pltpu.PrefetchScalarGridSpec(
            num_scalar_prefetch=0,
            grid=(b, nh // 2, t // tq),
            in_specs=[
                pl.BlockSpec(memory_space=pltpu.SMEM),
                pl.BlockSpec((1, 2, tq, dh), lambda i, h, j: (i, h, j, 0)),
                pl.BlockSpec((1, 2, dh, t), lambda i, h, j: (i, h, 0, 0)),
                pl.BlockSpec((1, 2, t, dh), lambda i, h, j: (i, h, 0, 0)),
                pl.BlockSpec((tq, tq), lambda i, h, j: (0, 0)),
            ],
            out_specs=pl.BlockSpec((1, tq, 2 * dh), lambda i, h, j: (i, j, h)),
            scratch_shapes=[pltpu.VMEM((2, 2, tq, tq), F32), pltpu.VMEM((2, tq, tq), BF16),
                            pltpu.VMEM((2, tq, dh), F32), pltpu.VMEM((2, tq, 1), F32)],
        ),
        out_shape=jax.ShapeDtypeStruct((b, t, nh * dh), BF16),
        compiler_params=_cparams(("parallel", "parallel", "arbitrary")),
    )(bias, q, kt, v, later)


def _sb_decode_kernel(pt_ref, q_ref, bias_ref, later_ref, *refs, n_pages):
    k_refs = refs[:n_pages]
    v_refs = refs[n_pages:2 * n_pages]
    o_ref = refs[2 * n_pages]
    head_of_lane = lax.broadcasted_iota(jnp.int32, (SB_HEADS, SB_WIDTH), 1) // SB_DH
    own = head_of_lane == lax.broadcasted_iota(jnp.int32, (SB_HEADS, SB_WIDTH), 0)
    qbd = jnp.where(own, q_ref[0], 0.0).astype(BF16)
    order = range(n_pages - 1, -1, -1)
    zs = [_dot(qbd, k_refs[p][0].astype(BF16)) + bias_ref[...] for p in order]
    ws, _ = _sb_blocks([zs], later_ref[...], [jnp.zeros((SB_HEADS, 1), F32)], None)
    acc = _dot_nt(ws[0][0], v_refs[order[0]][0].astype(BF16))
    for w, p in zip(ws[0][1:], order[1:]):
        acc = acc + _dot_nt(w, v_refs[p][0].astype(BF16))
    o_ref[0] = jnp.sum(jnp.where(own, acc, 0.0), axis=0, keepdims=True).astype(BF16)


def _sb_decode(q, cache_k, cache_v, page_table, bias_b, later, *, layer):
    n, n_pages = page_table.shape
    depth, n_pool, page, nh, dh = cache_k.shape
    ck = jnp.transpose(cache_k, (0, 1, 3, 4, 2)).reshape(depth * n_pool, nh * dh, page)
    cv = jnp.transpose(cache_v, (0, 1, 3, 4, 2)).reshape(depth * n_pool, nh * dh, page)

    def page_spec(p):
        return pl.BlockSpec((1, nh * dh, page), lambda i, pt: (layer * n_pool + pt[i * n_pages + p], 0, 0))

    row = lambda i, pt: (i, 0, 0)
    return pl.pallas_call(
        functools.partial(_sb_decode_kernel, n_pages=n_pages),
        name="sb_decode",
        grid_spec=pltpu.PrefetchScalarGridSpec(
            num_scalar_prefetch=1,
            grid=(n,),
            in_specs=[
                pl.BlockSpec((1, 1, nh * dh), row),
                pl.BlockSpec((nh, 128), lambda i, pt: (0, 0)),
                pl.BlockSpec((page, page), lambda i, pt: (0, 0)),
            ] + [page_spec(p) for p in range(n_pages)] * 2,
            out_specs=pl.BlockSpec((1, 1, nh * dh), row),
        ),
        out_shape=jax.ShapeDtypeStruct((n, 1, nh * dh), BF16),
        compiler_params=_cparams(("parallel",)),
    )(page_table.reshape(-1), q.reshape(n, 1, nh * dh), bias_b, later, *([ck] * n_pages), *([cv] * n_pages))


def _mlp_kernel(x_ref, oa_ref, ob_ref, wo_ref, nw_ref, wup_ref, wdn_ref, y_ref, x1_ref, xn_ref, acc_ref):
    j = pl.program_id(1)

    @pl.when(j == 0)
    def _():
        x1 = x_ref[...] + (_dot(oa_ref[...], wo_ref[:GDN_WIDTH]) + _dot(ob_ref[...], wo_ref[GDN_WIDTH:]))
        x1_ref[...] = x1
        ms = jnp.mean(x1 * x1, axis=-1, keepdims=True)
        xn_ref[...] = (x1 * lax.rsqrt(ms + EPS) * nw_ref[...]).astype(BF16)
        acc_ref[...] = jnp.zeros_like(acc_ref)

    h = jnp.maximum(_dot(xn_ref[...], wup_ref[...]), 0.0)
    acc_ref[...] += _dot((h * h).astype(BF16), wdn_ref[...])

    @pl.when(j == pl.num_programs(1) - 1)
    def _():
        y_ref[...] = x1_ref[...] + acc_ref[...]


def _mix_mlp(x, oa, ob, wo, nw, wup, wdn, *, tm, tf):
    m, d = x.shape
    ff = wup.shape[1]
    row = lambda i, j: (i, 0)
    const = lambda i, j: (0, 0)
    return pl.pallas_call(
        _mlp_kernel,
        name="mix_mlp",
        grid=(m // tm, ff // tf),
        in_specs=[
            pl.BlockSpec((tm, d), row),
            pl.BlockSpec((tm, GDN_WIDTH), row),
            pl.BlockSpec((tm, SB_WIDTH), row),
            pl.BlockSpec((d, d), const),
            pl.BlockSpec((1, d), const),
            pl.BlockSpec((d, tf), lambda i, j: (0, j)),
            pl.BlockSpec((tf, d), lambda i, j: (j, 0)),
        ],
        out_specs=pl.BlockSpec((tm, d), row),
        out_shape=jax.ShapeDtypeStruct((m, d), F32),
        scratch_shapes=[pltpu.VMEM((tm, d), F32), pltpu.VMEM((tm, d), BF16), pltpu.VMEM((tm, d), F32)],
        compiler_params=_cparams(("parallel", "arbitrary")),
    )(x, oa, ob, wo, nw, wup, wdn)


def _tri_consts(n_chunk_tile, chunk, tq, page):
    r = jnp.arange(n_chunk_tile)
    same = (r[:, None] // chunk) == (r[None, :] // chunk)
    ltri = (same & (r[None, :] <= r[:, None])).astype(BF16)

    def later(nk):
        k = jnp.arange(nk)
        return (k[:, None] > k[None, :]).astype(BF16)

    return ltri, ltri.T, later(tq), later(page)


def kernel(x_prompt, x_sample, cache_k, cache_v, page_table, state_gdn, state_conv, meta_tokens, norm1_w, w_in,
           conv_w, a_log, dt_bias, gdn_norm_w, q_norm_w, k_norm_w, sb_bias, w_o, norm2_w, w_up, w_down):
    depth, d_model = norm1_w.shape
    bp, seq, _ = x_prompt.shape
    bs = x_sample.shape[0]
    page = cache_k.shape[2]
    t_real = N_META + seq
    tp = -(-t_real // ROW_TILE) * ROW_TILE

    c0 = CONV_DIM
    c1 = c0 + GDN_WIDTH
    c2 = c1 + GDN_HEADS
    c3 = c2 + GDN_HEADS
    w_in_r = jnp.concatenate(
        [w_in[..., :c1], w_in[..., c3:], w_in[..., c1:c3],
         jnp.zeros((depth, d_model, AB_COLS - 2 * GDN_HEADS), w_in.dtype)], axis=-1).astype(BF16)
    w_o16, w_up16, w_dn16 = w_o.astype(BF16), w_up.astype(BF16), w_down.astype(BF16)
    pad_lanes = lambda a: jnp.pad(a, ((0, 0), (0, AB_COLS - a.shape[1])))[:, None, :]
    pa, pb = pad_lanes(a_log), pad_lanes(dt_bias)
    qnw = jnp.tile(q_norm_w, (1, SB_HEADS))[:, None, :]
    knw = jnp.tile(k_norm_w, (1, SB_HEADS))[:, None, :]
    lane_head = jnp.arange(SB_WIDTH) // SB_DH
    gseg = (lane_head[:, None] == lane_head[None, :]).astype(BF16)
    ltri, ltrit, later_q, later_p = _tri_consts(GDN_TILE, GDN_CHUNK, ROW_TILE, page)
    sb_bias2 = -sb_bias * LOG2E
    bias_b = jnp.broadcast_to(sb_bias2[:, :, None], (depth, SB_HEADS, 128))

    meta = jnp.broadcast_to(meta_tokens[None], (bp, N_META, d_model))
    xp = jnp.concatenate([meta, x_prompt, jnp.zeros((bp, tp - t_real, d_model), x_prompt.dtype)], axis=1)
    xs = x_sample.reshape(1, bs, d_model)
    st_conv = state_conv.reshape(depth, bs, (CONV_TAPS - 1) * CONV_DIM)

    kp_l, vp_l, sp_l, cp_l = [], [], [], []
    ks_l, vs_l, ss_l, cs_l = [], [], [], []
    for l in range(depth):
        n1, n2 = norm1_w[l][None], norm2_w[l][None]
        gnw = gdn_norm_w[l][None]
        conv_in, z, ab, q, kt, v, knew, vnew = _inproj(xp, n1, w_in_r[l], gseg, qnw[l], knw[l], prompt=True,
                                                      tm=ROW_TILE, t_out=t_real)
        u, gb, gbt = _conv_prompt(conv_in, conv_w[l], ab, pa[l], pb[l], t_real=t_real, tm=ROW_TILE)
        o_a, s_fin = _gdn_prompt(u, gb, gbt, z, gnw, ltri, ltrit)
        o_b = _sb_prompt(q, kt, v, sb_bias2[l], later_q, tq=ROW_TILE)
        xp = _mix_mlp(xp.reshape(bp * tp, d_model), o_a.reshape(bp * tp, GDN_WIDTH), o_b.reshape(bp * tp, SB_WIDTH),
                      w_o16[l], n2, w_up16[l], w_dn16[l], tm=512, tf=1024).reshape(bp, tp, d_model)
        kp_l.append(knew.reshape(bp, SB_HEADS, SB_DH, t_real).transpose(0, 3, 1, 2))
        vp_l.append(vnew.reshape(bp, SB_HEADS, SB_DH, t_real).transpose(0, 3, 1, 2))
        sp_l.append(s_fin)
        cp_l.append(conv_in[:, t_real - (CONV_TAPS - 1):t_real])

        conv_in, z, ab, q, knew, vnew = _inproj(xs, n1, w_in_r[l], gseg, qnw[l], knw[l], prompt=False, tm=bs)
        u, gb, ns = _conv_sample(conv_in[0], st_conv[l], conv_w[l], ab[0], pa[l], pb[l])
        s_new, o_a = _gdn_sample(state_gdn[l], u, gb, z[0], gnw)
        o_b = _sb_decode(q[0], cache_k, cache_v, page_table, bias_b[l], later_p, layer=l)
        xs = _mix_mlp(xs[0], o_a.reshape(bs, GDN_WIDTH), o_b.reshape(bs, SB_WIDTH), w_o16[l], n2, w_up16[l],
                      w_dn16[l], tm=bs, tf=1024)[None]
        ks_l.append(knew.reshape(bs, 1, SB_HEADS, SB_DH))
        vs_l.append(vnew.reshape(bs, 1, SB_HEADS, SB_DH))
        ss_l.append(s_new)
        cs_l.append(ns.reshape(bs, CONV_TAPS - 1, CONV_DIM))

    y_prompt = xp[:, N_META:t_real]
    return (y_prompt, xs.reshape(bs, 1, d_model),
            jnp.stack(kp_l), jnp.stack(vp_l), jnp.stack(sp_l), jnp.stack(cp_l),
            jnp.stack(ks_l), jnp.stack(vs_l), jnp.stack(ss_l), jnp.stack(cs_l))
```

```python
import functools
import math

import jax
import jax.numpy as jnp
from jax import lax
from jax.experimental import pallas as pl
from jax.experimental.pallas import tpu as pltpu

F32 = jnp.float32
BF16 = jnp.bfloat16
EPS = 1e-6
LOG2E = 1.4426950408889634

N_META = 16
GDN_HEADS = 4
GDN_D = 128
GDN_WIDTH = GDN_HEADS * GDN_D
CONV_DIM = 3 * GDN_WIDTH
CONV_TAPS = 4
GDN_CHUNK = 64
SB_HEADS = 8
SB_DH = 64
SB_WIDTH = SB_HEADS * SB_DH
AB_COLS = 128
C_CONV = 0
C_Z = C_CONV + CONV_DIM
C_SQ = C_Z + GDN_WIDTH
C_SK = C_SQ + SB_WIDTH
C_SV = C_SK + SB_WIDTH
C_AB = C_SV + SB_WIDTH
IN_COLS_PAD = C_AB + AB_COLS

BF16_ROWS = 16
ROW_TILE = 256
GDN_TILE = 2 * GDN_CHUNK
GDN_BATCH_TILE = 4
GDN_SAMPLE_TILE = 4
VMEM_LIMIT = 56 * 1024 * 1024


def _cparams(sem):
    return pltpu.CompilerParams(dimension_semantics=sem, vmem_limit_bytes=VMEM_LIMIT)


def _dot(a, b):
    return jnp.dot(a, b, preferred_element_type=F32)


def _dot_nt(a, b):
    return lax.dot_general(a, b, (((1,), (1,)), ((), ())), preferred_element_type=F32)


def _split2(x):
    hi = x.astype(BF16)
    lo = (x - hi.astype(F32)).astype(BF16)
    return hi, lo


def _split3(x):
    hi = x.astype(BF16)
    r = x - hi.astype(F32)
    mid = r.astype(BF16)
    lo = (r - mid.astype(F32)).astype(BF16)
    return hi, mid, lo


def _dot3(a, b):
    ah, al = _split2(a)
    bh, bl = _split2(b)
    return _dot(ah, bh) + (_dot(ah, bl) + _dot(al, bh))


def _sigmoid(x):
    return 1.0 / (1.0 + jnp.exp(-x))


def _softplus(x):
    return jnp.maximum(x, 0.0) + jnp.log(1.0 + jnp.exp(-jnp.abs(x)))


def _inproj_kernel(x_ref, nw_ref, w_ref, gseg_ref, qnw_ref, knw_ref, *outs, prompt):
    x = x_ref[0]
    ms = jnp.mean(x * x, axis=-1, keepdims=True)
    h = (x * lax.rsqrt(ms + EPS) * nw_ref[...]).astype(BF16)

    def proj(lo, hi):
        return _dot(h, w_ref[:, lo:hi])

    def headnorm(s, wrow):
        hi, lo = _split2(s * s)
        ss = _dot(hi, gseg_ref[...]) + _dot(lo, gseg_ref[...])
        return s * lax.rsqrt(ss * (1.0 / SB_DH) + EPS) * wrow

    qn = headnorm(proj(C_SQ, C_SK), qnw_ref[...]) * (-(SB_DH ** -0.5) * LOG2E)
    kn = headnorm(proj(C_SK, C_SV), knw_ref[...])
    sv = proj(C_SV, C_AB)
    if prompt:
        conv_ref, z_ref, ab_ref, q_ref, kt_ref, v_ref, knew_ref, vnew_ref = outs
        for hd in range(SB_HEADS):
            q_ref[0, hd] = qn[:, hd * SB_DH:(hd + 1) * SB_DH].astype(BF16)
            v_ref[0, hd] = sv[:, hd * SB_DH:(hd + 1) * SB_DH].astype(BF16)
        knt = kn.T
        kt_ref[0] = knt.astype(BF16)
        knew_ref[0] = knt
        vnew_ref[0] = sv.T
    else:
        conv_ref, z_ref, ab_ref, q_ref, knew_ref, vnew_ref = outs
        q_ref[0] = qn
        knew_ref[0] = kn
        vnew_ref[0] = sv
    conv_ref[0] = proj(C_CONV, C_Z)
    z_ref[0] = proj(C_Z, C_SQ)
    ab_ref[0] = proj(C_AB, IN_COLS_PAD)


def _inproj(x, nw, w, gseg, qnw, knw, *, prompt, tm, t_out=None):
    b, t, d = x.shape
    grid = (b, t // tm)
    row = lambda i, j: (i, j, 0)
    const = lambda i, j: (0, 0)
    in_specs = [
        pl.BlockSpec((1, tm, d), row),
        pl.BlockSpec((1, d), const),
        pl.BlockSpec((d, IN_COLS_PAD), const),
        pl.BlockSpec((SB_WIDTH, SB_WIDTH), const),
        pl.BlockSpec((1, SB_WIDTH), const),
        pl.BlockSpec((1, SB_WIDTH), const),
    ]
    f32_rows = lambda n: (jax.ShapeDtypeStruct((b, t, n), F32), pl.BlockSpec((1, tm, n), row))
    outs = [f32_rows(CONV_DIM), f32_rows(GDN_WIDTH), f32_rows(AB_COLS)]
    if prompt:
        hm = lambda i, j: (i, 0, j, 0)
        outs += [
            (jax.ShapeDtypeStruct((b, SB_HEADS, t, SB_DH), BF16), pl.BlockSpec((1, SB_HEADS, tm, SB_DH), hm)),
            (jax.ShapeDtypeStruct((b, SB_WIDTH, t), BF16), pl.BlockSpec((1, SB_WIDTH, tm), lambda i, j: (i, 0, j))),
            (jax.ShapeDtypeStruct((b, SB_HEADS, t, SB_DH), BF16), pl.BlockSpec((1, SB_HEADS, tm, SB_DH), hm)),
        ]
        cols = lambda i, j: (i, 0, j)
        outs += [(jax.ShapeDtypeStruct((b, SB_WIDTH, t_out), F32), pl.BlockSpec((1, SB_WIDTH, tm), cols))] * 2
    else:
        outs += [f32_rows(SB_WIDTH)] * 3
    return pl.pallas_call(
        functools.partial(_inproj_kernel, prompt=prompt),
        name="inproj_prompt" if prompt else "inproj_sample",
        grid=grid,
        in_specs=in_specs,
        out_specs=[o[1] for o in outs],
        out_shape=[o[0] for o in outs],
        compiler_params=_cparams(("parallel", "parallel")),
    )(x, nw, w, gseg, qnw, knw)


def _gdn_inputs(conv_out, ab, pa, pb, valid):
    u = conv_out * _sigmoid(conv_out)
    parts = []
    for hd in range(2 * GDN_HEADS):
        sl = u[:, hd * GDN_D:(hd + 1) * GDN_D]
        nrm = sl * lax.rsqrt(jnp.sum(sl * sl, axis=-1, keepdims=True) + EPS)
        parts.append(nrm * (GDN_D ** -0.5) if hd < GDN_HEADS else nrm)
    parts.append(u[:, 2 * GDN_WIDTH:])
    g = -jnp.exp(pa) * _softplus(ab + pb)
    beta = _sigmoid(ab)
    lane = lax.broadcasted_iota(jnp.int32, ab.shape, 1)
    gb = jnp.where(lane < GDN_HEADS, g, jnp.where(lane < 2 * GDN_HEADS, beta, 0.0))
    if valid is not None:
        gb = jnp.where(valid, gb, 0.0)
    return jnp.concatenate(parts, axis=1), gb


def _conv_prompt_kernel(x_ref, prev_ref, cw_ref, ab_ref, pa_ref, pb_ref, u_ref, gb_ref, gbt_ref, xs_ref, *, t_real):
    i = pl.program_id(1)
    tm = x_ref.shape[1]
    xs_ref[0:8] = jnp.where(i > 0, prev_ref[0], 0.0)
    xs_ref[8:] = x_ref[0]
    out = xs_ref[pl.ds(5, tm)] * cw_ref[0:1]
    for tap in range(1, CONV_TAPS):
        out = out + xs_ref[pl.ds(5 + tap, tm)] * cw_ref[tap:tap + 1]
    row = i * tm + lax.broadcasted_iota(jnp.int32, (tm, AB_COLS), 0)
    u, gb = _gdn_inputs(out, ab_ref[0], pa_ref[...], pb_ref[...], row < t_real)
    u_ref[0] = u
    gb_ref[0] = gb
    gbt_ref[0] = gb.T[0:8]


def _conv_prompt(conv_in, cw, ab, pa, pb, *, t_real, tm):
    b, t, c = conv_in.shape
    row = lambda i, j: (i, j, 0)
    const = lambda i, j: (0, 0)
    return pl.pallas_call(
        functools.partial(_conv_prompt_kernel, t_real=t_real),
        name="conv_prompt",
        grid=(b, t // tm),
        in_specs=[
            pl.BlockSpec((1, tm, c), row),
            pl.BlockSpec((1, 8, c), lambda i, j: (i, jnp.maximum(j * (tm // 8) - 1, 0), 0)),
            pl.BlockSpec((CONV_TAPS, c), const),
            pl.BlockSpec((1, tm, AB_COLS), row),
            pl.BlockSpec((1, AB_COLS), const),
            pl.BlockSpec((1, AB_COLS), const),
        ],
        out_specs=[
            pl.BlockSpec((1, tm, c), row),
            pl.BlockSpec((1, tm, AB_COLS), row),
            pl.BlockSpec((1, 8, tm), lambda i, j: (i, 0, j)),
        ],
        out_shape=[
            jax.ShapeDtypeStruct((b, t, c), F32),
            jax.ShapeDtypeStruct((b, t, AB_COLS), F32),
            jax.ShapeDtypeStruct((b, 8, t), F32),
        ],
        scratch_shapes=[pltpu.VMEM((tm + 8, c), F32)],
        compiler_params=_cparams(("parallel", "parallel")),
    )(conv_in, conv_in, cw, ab, pa, pb)


def _conv_sample_kernel(x_ref, st_ref, cw_ref, ab_ref, pa_ref, pb_ref, u_ref, gb_ref, ns_ref):
    c = x_ref.shape[1]
    taps = [st_ref[:, k * c:(k + 1) * c] for k in range(CONV_TAPS - 1)] + [x_ref[...]]
    out = taps[0] * cw_ref[0:1]
    for tap in range(1, CONV_TAPS):
        out = out + taps[tap] * cw_ref[tap:tap + 1]
    u, gb = _gdn_inputs(out, ab_ref[...], pa_ref[...], pb_ref[...], None)
    u_ref[...] = u
    gb_ref[...] = gb
    for k in range(CONV_TAPS - 1):
        ns_ref[:, k * c:(k + 1) * c] = taps[k + 1]


def _conv_sample(conv_in, state, cw, ab, pa, pb):
    n, c = conv_in.shape
    return pl.pallas_call(
        _conv_sample_kernel,
        name="conv_sample",
        out_shape=[
            jax.ShapeDtypeStruct((n, c), F32),
            jax.ShapeDtypeStruct((n, AB_COLS), F32),
            jax.ShapeDtypeStruct((n, (CONV_TAPS - 1) * c), F32),
        ],
        compiler_params=pltpu.CompilerParams(vmem_limit_bytes=VMEM_LIMIT),
    )(conv_in, state, cw, ab, pa, pb)


def _gated_norm(o, z, nw):
    return o * lax.rsqrt(jnp.mean(o * o, axis=-1, keepdims=True) + EPS) * nw * (z * _sigmoid(z))


def _gdn_chunk_kernel(u_ref, gb_ref, gbt_ref, z_ref, nw_ref, ltri_ref, ltrit_ref, o_ref, sout_ref, s_ref):
    step = pl.program_id(1)
    n = GDN_TILE
    c = GDN_CHUNK

    @pl.when(step == 0)
    def _():
        s_ref[...] = jnp.zeros_like(s_ref)

    ltri = ltri_ref[...]
    ltrit = ltrit_ref[...]
    batch = range(u_ref.shape[0])
    gb = [gb_ref[bi] for bi in batch]
    gparts = [_split3(gb[bi]) for bi in batch]
    tparts = [_split3(gbt_ref[bi]) for bi in batch]
    cum_cols = [_dot(ltri, p[0]) + (_dot(ltri, p[1]) + _dot(ltri, p[2])) for p in gparts]
    cum_rows = [_dot(p[0], ltrit) + (_dot(p[1], ltrit) + _dot(p[2], ltrit)) for p in tparts]

    row = lax.broadcasted_iota(jnp.int32, (n, n), 0)
    col = lax.broadcasted_iota(jnp.int32, (n, n), 1)
    same = (row >= c) == (col >= c)
    causal = same & (row >= col)
    strict = same & (row > col)
    eye = (row == col).astype(F32)
    second = lax.broadcasted_iota(jnp.int32, (n, 1), 0) >= c
    zeros_c = jnp.zeros((c, GDN_D), F32)

    chains = [(bi, hd) for bi in batch for hd in range(GDN_HEADS)]
    ids = range(len(chains))
    lanes = lambda hd, base: slice(base + hd * GDN_D, base + (hd + 1) * GDN_D)
    q = [u_ref[bi, :, lanes(hd, 0)] for bi, hd in chains]
    k = [u_ref[bi, :, lanes(hd, GDN_WIDTH)] for bi, hd in chains]
    v = [u_ref[bi, :, lanes(hd, 2 * GDN_WIDTH)] for bi, hd in chains]
    cc = [cum_cols[bi][:, hd:hd + 1] for bi, hd in chains]
    cr = [cum_rows[bi][hd:hd + 1, :] for bi, hd in chains]
    beta = [gb[bi][:, GDN_HEADS + hd:GDN_HEADS + hd + 1] for bi, hd in chains]
    gam = [jnp.where(causal, jnp.exp(jnp.where(causal, cc[x] - cr[x], 0.0)), 0.0) for x in ids]
    kb = [k[x] * beta[x] for x in ids]
    k16 = [k[x].astype(BF16) for x in ids]
    a_mat = [jnp.where(strict, _dot_nt(kb[x].astype(BF16), k16[x]) * gam[x], 0.0) for x in ids]
    qk = [(_dot_nt(q[x].astype(BF16), k16[x]) * gam[x]).astype(BF16) for x in ids]
    inv = [eye - a_mat[x] for x in ids]
    pw = [_dot3(a_mat[x], a_mat[x]) for x in ids]
    for _ in range(4):
        both = [_dot3(jnp.concatenate([inv[x], pw[x]], axis=0), pw[x]) for x in ids]
        inv = [inv[x] + both[x][:n] for x in ids]
        pw = [both[x][n:] for x in ids]
    inv = [inv[x] + _dot3(inv[x], pw[x]) for x in ids]
    ecum = [jnp.exp(cc[x]) for x in ids]
    uw = [_dot3(inv[x], jnp.concatenate([v[x] * beta[x], kb[x] * ecum[x]], axis=1)) for x in ids]
    uu = [uw[x][:, :GDN_D] for x in ids]
    ww = [uw[x][:, GDN_D:].astype(BF16) for x in ids]
    qg = [(q[x] * ecum[x]).astype(BF16) for x in ids]
    g_last0 = [cc[x][c - 1:c] for x in ids]
    g_last1 = [cc[x][n - 1:n] for x in ids]
    kdt = [(k[x] * jnp.exp(jnp.where(second, g_last1[x], g_last0[x]) - cc[x])).T.astype(BF16) for x in ids]

    s0 = [s_ref[bi, hd] for bi, hd in chains]
    s0b = [s0[x].astype(BF16) for x in ids]
    vn0 = [uu[x][:c] - _dot(ww[x][:c], s0b[x]) for x in ids]
    oi0 = [_dot(qg[x][:c], s0b[x]) for x in ids]
    s1 = [s0[x] * jnp.exp(g_last0[x]) + _dot(kdt[x], jnp.concatenate([vn0[x], zeros_c], axis=0).astype(BF16))
          for x in ids]
    s1b = [s1[x].astype(BF16) for x in ids]
    vn1 = [uu[x][c:] - _dot(ww[x][c:], s1b[x]) for x in ids]
    oi1 = [_dot(qg[x][c:], s1b[x]) for x in ids]
    s2 = [s1[x] * jnp.exp(g_last1[x]) + _dot(kdt[x], jnp.concatenate([zeros_c, vn1[x]], axis=0).astype(BF16))
          for x in ids]
    for x, (bi, hd) in enumerate(chains):
        s_ref[bi, hd] = s2[x]
        vn = jnp.concatenate([vn0[x], vn1[x]], axis=0).astype(BF16)
        o = jnp.concatenate([oi0[x], oi1[x]], axis=0) + _dot(qk[x], vn)
        o_ref[bi, :, lanes(hd, 0)] = _gated_norm(o, z_ref[bi, :, lanes(hd, 0)], nw_ref[...]).astype(BF16)

    @pl.when(step == pl.num_programs(1) - 1)
    def _():
        sout_ref[...] = s_ref[...]


def _gdn_prompt(u, gb, gbt, z, nw, ltri, ltrit):
    b, t, _ = u.shape
    n = GDN_TILE
    nb = math.gcd(b, GDN_BATCH_TILE)
    row = lambda i, j: (i, j, 0)
    const = lambda i, j: (0, 0)
    return pl.pallas_call(
        _gdn_chunk_kernel,
        name="gdn_prompt",
        grid=(b // nb, t // n),
        in_specs=[
            pl.BlockSpec((nb, n, CONV_DIM), row),
            pl.BlockSpec((nb, n, AB_COLS), row),
            pl.BlockSpec((nb, 8, n), lambda i, j: (i, 0, j)),
            pl.BlockSpec((nb, n, GDN_WIDTH), row),
            pl.BlockSpec((1, GDN_D), const),
            pl.BlockSpec((n, n), const),
            pl.BlockSpec((n, n), const),
        ],
        out_specs=[
            pl.BlockSpec((nb, n, GDN_WIDTH), row),
            pl.BlockSpec((nb, GDN_HEADS, GDN_D, GDN_D), lambda i, j: (i, 0, 0, 0)),
        ],
        out_shape=[
            jax.ShapeDtypeStruct((b, t, GDN_WIDTH), BF16),
            jax.ShapeDtypeStruct((b, GDN_HEADS, GDN_D, GDN_D), F32),
        ],
        scratch_shapes=[pltpu.VMEM((nb, GDN_HEADS, GDN_D, GDN_D), F32)],
        compiler_params=_cparams(("parallel", "arbitrary")),
    )(u, gb, gbt, z, nw, ltri, ltrit)


def _gdn_step_kernel(s_ref, u_ref, gb_ref, z_ref, nw_ref, sout_ref, o_ref):
    row = lax.broadcasted_iota(jnp.int32, (GDN_D, GDN_D), 0)
    col = lax.broadcasted_iota(jnp.int32, (GDN_D, GDN_D), 1)
    eye = row == col

    def column(r):
        return jnp.sum(jnp.where(eye, r, 0.0), axis=1, keepdims=True)

    for bi in range(s_ref.shape[0]):
        for hd in range(GDN_HEADS):
            q = u_ref[bi, :, hd * GDN_D:(hd + 1) * GDN_D]
            k = u_ref[bi, :, GDN_WIDTH + hd * GDN_D:GDN_WIDTH + (hd + 1) * GDN_D]
            v = u_ref[bi, :, 2 * GDN_WIDTH + hd * GDN_D:2 * GDN_WIDTH + (hd + 1) * GDN_D]
            g = gb_ref[bi, :, hd:hd + 1]
            beta = gb_ref[bi, :, GDN_HEADS + hd:GDN_HEADS + hd + 1]
            kc = column(k)
            s = s_ref[bi, hd] * jnp.exp(g)
            err = v - jnp.sum(kc * s, axis=0, keepdims=True)
            s = s + kc * (err * beta)
            sout_ref[bi, hd] = s
            o = jnp.sum(column(q) * s, axis=0, keepdims=True)
            zg = z_ref[bi, :, hd * GDN_D:(hd + 1) * GDN_D]
            o_ref[bi, :, hd * GDN_D:(hd + 1) * GDN_D] = _gated_norm(o, zg, nw_ref[...]).astype(BF16)


def _gdn_sample(state, u, gb, z, nw):
    n = state.shape[0]
    bt = GDN_SAMPLE_TILE
    row = lambda i: (i, 0, 0)
    st = lambda i: (i, 0, 0, 0)
    return pl.pallas_call(
        _gdn_step_kernel,
        name="gdn_sample",
        grid=(n // bt,),
        in_specs=[
            pl.BlockSpec((bt, GDN_HEADS, GDN_D, GDN_D), st),
            pl.BlockSpec((bt, 1, CONV_DIM), row),
            pl.BlockSpec((bt, 1, AB_COLS), row),
            pl.BlockSpec((bt, 1, GDN_WIDTH), row),
            pl.BlockSpec((1, GDN_D), lambda i: (0, 0)),
        ],
        out_specs=[
            pl.BlockSpec((bt, GDN_HEADS, GDN_D, GDN_D), st),
            pl.BlockSpec((bt, 1, GDN_WIDTH), row),
        ],
        out_shape=[
            jax.ShapeDtypeStruct(state.shape, F32),
            jax.ShapeDtypeStruct((n, 1, GDN_WIDTH), BF16),
        ],
        compiler_params=_cparams(("parallel",)),
    )(state, u.reshape(n, 1, CONV_DIM), gb.reshape(n, 1, AB_COLS), z.reshape(n, 1, GDN_WIDTH), nw)


def _neg_abs(x):
    bits = lax.bitcast_convert_type(x, jnp.uint32) | jnp.uint32(0x80000000)
    return lax.bitcast_convert_type(bits, F32)


def _sb_blocks(zns, later, carries, mask):
    stays = [[jnp.minimum(zn, 0.0) - jnp.log2(1.0 + jnp.exp2(_neg_abs(zn))) for zn in ch] for ch in zns]
    if mask is not None:
        stays = [[jnp.where(mask, s, 0.0) for s in ch] for ch in stays]
    scans = [[_dot(s.astype(BF16), later) for s in ch] for ch in stays]
    sums = [[jnp.sum(s, axis=1, keepdims=True) for s in ch] for ch in stays]
    ws, out_carries = [], []
    for c, carry in enumerate(carries):
        ws.append([])
        for zn, stay, scan, rs in zip(zns[c], stays[c], scans[c], sums[c]):
            w = jnp.exp2((stay - zn) + (scan + carry))
            if mask is not None:
                w = jnp.where(mask, w, 0.0)
            ws[c].append(w.astype(BF16))
            carry = carry + rs
        out_carries.append(carry)
    return ws, out_carries


def _sb_prompt_kernel(bias_ref, q_ref, kt_ref, v_ref, later_ref, o_ref, zn_ref, w_ref, acc_ref, car_ref, *,
                      last_rows):
    hp = pl.program_id(1)
    i = pl.program_id(2)
    nq = q_ref.shape[2]
    heads = range(2)

    def span(blk):
        return pl.ds(pl.multiple_of(blk * nq, nq), nq)

    def run(nr):
        rows = slice(0, nr)
        diag_mask = lax.broadcasted_iota(jnp.int32, (nr, nq), 1) < lax.broadcasted_iota(jnp.int32, (nr, nq), 0)

        def logits(blk, slot):
            for hh in heads:
                zn_ref[slot, hh, rows] = _dot(q_ref[0, hh, rows], kt_ref[0, hh, :, span(blk)]) + bias_ref[2 * hp + hh]

        def weights(slot, mask):
            ws, carries = _sb_blocks([[zn_ref[slot, hh, rows]] for hh in heads], later_ref[...],
                                     [car_ref[hh, rows] for hh in heads], mask)
            for hh in heads:
                w_ref[hh, rows] = ws[hh][0]
                car_ref[hh, rows] = carries[hh]

        def attend(blk):
            for hh in heads:
                acc_ref[hh, rows] += _dot(w_ref[hh, rows], v_ref[0, hh, span(blk), :])

        def step(s, slot):
            attend(i - s + 1)
            logits(jnp.maximum(i - s - 1, 0), 1 - slot)
            weights(slot, None)

        acc_ref[...] = jnp.zeros_like(acc_ref)
        car_ref[...] = jnp.zeros_like(car_ref)
        logits(i, 0)
        logits(jnp.maximum(i - 1, 0), 1)
        weights(0, diag_mask)

        def body(p, _):
            step(2 * p + 1, 1)
            step(2 * p + 2, 0)
            return 0

        lax.fori_loop(0, i // 2, body, 0)

        @pl.when(i % 2 == 1)
        def _():
            step(i, 1)

        attend(0)
        o_ref[0] = jnp.concatenate([acc_ref[hh] for hh in heads], axis=1).astype(BF16)

    if last_rows == nq:
        run(nq)
    else:
        last = pl.num_programs(2) - 1
        pl.when(i < last)(lambda: run(nq))
        pl.when(i == last)(lambda: run(last_rows))


def _sb_prompt(q, kt, v, bias, later, *, tq, t_real):
    b, nh, t, dh = q.shape
    kt = kt.reshape(b, nh, dh, t)
    last_rows = min(tq, -(-(t_real - (t // tq - 1) * tq) // BF16_ROWS) * BF16_ROWS)
    return pl.pallas_call(
        functools.partial(_sb_prompt_kernel, last_rows=last_rows),
        name="sb_prompt",
        grid_spec=pltpu.PrefetchScalarGridSpec(
            num_scalar_prefetch=0,
            grid=(b, nh // 2, t // tq),
            in_specs=[
                pl.BlockSpec(memory_space=pltpu.SMEM),
                pl.BlockSpec((1, 2, tq, dh), lambda i, h, j: (i, h, j, 0)),
                pl.BlockSpec((1, 2, dh, t), lambda i, h, j: (i, h, 0, 0)),
                pl.BlockSpec((1, 2, t, dh), lambda i, h, j: (i, h, 0, 0)),
                pl.BlockSpec((tq, tq), lambda i, h, j: (0, 0)),
            ],
            out_specs=pl.BlockSpec((1, tq, 2 * dh), lambda i, h, j: (i, j, h)),
            scratch_shapes=[pltpu.VMEM((2, 2, tq, tq), F32), pltpu.VMEM((2, tq, tq), BF16),
                            pltpu.VMEM((2, tq, dh), F32), pltpu.VMEM((2, tq, 1), F32)],
        ),
        out_shape=jax.ShapeDtypeStruct((b, t, nh * dh), BF16),
        compiler_params=_cparams(("parallel", "parallel", "arbitrary")),
    )(bias, q, kt, v, later)


def _sb_decode_kernel(pt_ref, q_ref, bias_ref, later_ref, *refs, n_pages):
    k_refs = refs[:n_pages]
    v_refs = refs[n_pages:2 * n_pages]
    o_ref = refs[2 * n_pages]
    head_of_lane = lax.broadcasted_iota(jnp.int32, (SB_HEADS, SB_WIDTH), 1) // SB_DH
    own = head_of_lane == lax.broadcasted_iota(jnp.int32, (SB_HEADS, SB_WIDTH), 0)
    qbd = jnp.where(own, q_ref[0], 0.0).astype(BF16)
    order = range(n_pages - 1, -1, -1)
    zs = [_dot(qbd, k_refs[p][0].astype(BF16)) + bias_ref[...] for p in order]
    ws, _ = _sb_blocks([zs], later_ref[...], [jnp.zeros((SB_HEADS, 1), F32)], None)
    acc = _dot_nt(ws[0][0], v_refs[order[0]][0].astype(BF16))
    for w, p in zip(ws[0][1:], order[1:]):
        acc = acc + _dot_nt(w, v_refs[p][0].astype(BF16))
    o_ref[0] = jnp.sum(jnp.where(own, acc, 0.0), axis=0, keepdims=True).astype(BF16)


def _sb_decode(q, cache_k, cache_v, page_table, bias_b, later, *, layer):
    n, n_pages = page_table.shape
    depth, n_pool, page, nh, dh = cache_k.shape
    ck = jnp.transpose(cache_k, (0, 1, 3, 4, 2)).reshape(depth * n_pool, nh * dh, page)
    cv = jnp.transpose(cache_v, (0, 1, 3, 4, 2)).reshape(depth * n_pool, nh * dh, page)

    def page_spec(p):
        return pl.BlockSpec((1, nh * dh, page), lambda i, pt: (layer * n_pool + pt[i * n_pages + p], 0, 0))

    row = lambda i, pt: (i, 0, 0)
    return pl.pallas_call(
        functools.partial(_sb_decode_kernel, n_pages=n_pages),
        name="sb_decode",
        grid_spec=pltpu.PrefetchScalarGridSpec(
            num_scalar_prefetch=1,
            grid=(n,),
            in_specs=[
                pl.BlockSpec((1, 1, nh * dh), row),
                pl.BlockSpec((nh, 128), lambda i, pt: (0, 0)),
                pl.BlockSpec((page, page), lambda i, pt: (0, 0)),
            ] + [page_spec(p) for p in range(n_pages)] * 2,
            out_specs=pl.BlockSpec((1, 1, nh * dh), row),
        ),
        out_shape=jax.ShapeDtypeStruct((n, 1, nh * dh), BF16),
        compiler_params=_cparams(("parallel",)),
    )(page_table.reshape(-1), q.reshape(n, 1, nh * dh), bias_b, later, *([ck] * n_pages), *([cv] * n_pages))


def _mlp_kernel(x_ref, oa_ref, ob_ref, wo_ref, nw_ref, wup_ref, wdn_ref, y_ref, x1_ref, xn_ref, acc_ref):
    j = pl.program_id(1)

    @pl.when(j == 0)
    def _():
        x1 = x_ref[...] + (_dot(oa_ref[...], wo_ref[:GDN_WIDTH]) + _dot(ob_ref[...], wo_ref[GDN_WIDTH:]))
        x1_ref[...] = x1
        ms = jnp.mean(x1 * x1, axis=-1, keepdims=True)
        xn_ref[...] = (x1 * lax.rsqrt(ms + EPS) * nw_ref[...]).astype(BF16)
        acc_ref[...] = jnp.zeros_like(acc_ref)

    h = jnp.maximum(_dot(xn_ref[...], wup_ref[...]), 0.0)
    acc_ref[...] += _dot((h * h).astype(BF16), wdn_ref[...])

    @pl.when(j == pl.num_programs(1) - 1)
    def _():
        y_ref[...] = x1_ref[...] + acc_ref[...]


def _mix_mlp(x, oa, ob, wo, nw, wup, wdn, *, tm, tf):
    m, d = x.shape
    ff = wup.shape[1]
    row = lambda i, j: (i, 0)
    const = lambda i, j: (0, 0)
    return pl.pallas_call(
        _mlp_kernel,
        name="mix_mlp",
        grid=(m // tm, ff // tf),
        in_specs=[
            pl.BlockSpec((tm, d), row),
            pl.BlockSpec((tm, GDN_WIDTH), row),
            pl.BlockSpec((tm, SB_WIDTH), row),
            pl.BlockSpec((d, d), const),
            pl.BlockSpec((1, d), const),
            pl.BlockSpec((d, tf), lambda i, j: (0, j)),
            pl.BlockSpec((tf, d), lambda i, j: (j, 0)),
        ],
        out_specs=pl.BlockSpec((tm, d), row),
        out_shape=jax.ShapeDtypeStruct((m, d), F32),
        scratch_shapes=[pltpu.VMEM((tm, d), F32), pltpu.VMEM((tm, d), BF16), pltpu.VMEM((tm, d), F32)],
        compiler_params=_cparams(("parallel", "arbitrary")),
    )(x, oa, ob, wo, nw, wup, wdn)


def _tri_consts(n_chunk_tile, chunk, tq, page):
    r = jnp.arange(n_chunk_tile)
    same = (r[:, None] // chunk) == (r[None, :] // chunk)
    ltri = (same & (r[None, :] <= r[:, None])).astype(BF16)

    def later(nk):
        k = jnp.arange(nk)
        return (k[:, None] > k[None, :]).astype(BF16)

    return ltri, ltri.T, later(tq), later(page)


def kernel(x_prompt, x_sample, cache_k, cache_v, page_table, state_gdn, state_conv, meta_tokens, norm1_w, w_in,
           conv_w, a_log, dt_bias, gdn_norm_w, q_norm_w, k_norm_w, sb_bias, w_o, norm2_w, w_up, w_down):
    depth, d_model = norm1_w.shape
    bp, seq, _ = x_prompt.shape
    bs = x_sample.shape[0]
    page = cache_k.shape[2]
    t_real = N_META + seq
    tp = -(-t_real // ROW_TILE) * ROW_TILE

    c0 = CONV_DIM
    c1 = c0 + GDN_WIDTH
    c2 = c1 + GDN_HEADS
    c3 = c2 + GDN_HEADS
    w_in_r = jnp.concatenate(
        [w_in[..., :c1], w_in[..., c3:], w_in[..., c1:c3],
         jnp.zeros((depth, d_model, AB_COLS - 2 * GDN_HEADS), w_in.dtype)], axis=-1).astype(BF16)
    w_o16, w_up16, w_dn16 = w_o.astype(BF16), w_up.astype(BF16), w_down.astype(BF16)
    pad_lanes = lambda a: jnp.pad(a, ((0, 0), (0, AB_COLS - a.shape[1])))[:, None, :]
    pa, pb = pad_lanes(a_log), pad_lanes(dt_bias)
    qnw = jnp.tile(q_norm_w, (1, SB_HEADS))[:, None, :]
    knw = jnp.tile(k_norm_w, (1, SB_HEADS))[:, None, :]
    lane_head = jnp.arange(SB_WIDTH) // SB_DH
    gseg = (lane_head[:, None] == lane_head[None, :]).astype(BF16)
    ltri, ltrit, later_q, later_p = _tri_consts(GDN_TILE, GDN_CHUNK, ROW_TILE, page)
    sb_bias2 = -sb_bias * LOG2E
    bias_b = jnp.broadcast_to(sb_bias2[:, :, None], (depth, SB_HEADS, 128))

    meta = jnp.broadcast_to(meta_tokens[None], (bp, N_META, d_model))
    xp = jnp.concatenate([meta, x_prompt, jnp.zeros((bp, tp - t_real, d_model), x_prompt.dtype)], axis=1)
    xs = x_sample.reshape(1, bs, d_model)
    st_conv = state_conv.reshape(depth, bs, (CONV_TAPS - 1) * CONV_DIM)

    kp_l, vp_l, sp_l, cp_l = [], [], [], []
    ks_l, vs_l, ss_l, cs_l = [], [], [], []
    for l in range(depth):
        n1, n2 = norm1_w[l][None], norm2_w[l][None]
        gnw = gdn_norm_w[l][None]
        conv_in, z, ab, q, kt, v, knew, vnew = _inproj(xp, n1, w_in_r[l], gseg, qnw[l], knw[l], prompt=True,
                                                      tm=ROW_TILE, t_out=t_real)
        u, gb, gbt = _conv_prompt(conv_in, conv_w[l], ab, pa[l], pb[l], t_real=t_real, tm=ROW_TILE)
        o_a, s_fin = _gdn_prompt(u, gb, gbt, z, gnw, ltri, ltrit)
        o_b = _sb_prompt(q, kt, v, sb_bias2[l], later_q, tq=ROW_TILE, t_real=t_real)
        xp = _mix_mlp(xp.reshape(bp * tp, d_model), o_a.reshape(bp * tp, GDN_WIDTH), o_b.reshape(bp * tp, SB_WIDTH),
                      w_o16[l], n2, w_up16[l], w_dn16[l], tm=1024, tf=1024).reshape(bp, tp, d_model)
        kp_l.append(knew.reshape(bp, SB_HEADS, SB_DH, t_real).transpose(0, 3, 1, 2))
        vp_l.append(vnew.reshape(bp, SB_HEADS, SB_DH, t_real).transpose(0, 3, 1, 2))
        sp_l.append(s_fin)
        cp_l.append(conv_in[:, t_real - (CONV_TAPS - 1):t_real])

        conv_in, z, ab, q, knew, vnew = _inproj(xs, n1, w_in_r[l], gseg, qnw[l], knw[l], prompt=False, tm=bs)
        u, gb, ns = _conv_sample(conv_in[0], st_conv[l], conv_w[l], ab[0], pa[l], pb[l])
        s_new, o_a = _gdn_sample(state_gdn[l], u, gb, z[0], gnw)
        o_b = _sb_decode(q[0], cache_k, cache_v, page_table, bias_b[l], later_p, layer=l)
        xs = _mix_mlp(xs[0], o_a.reshape(bs, GDN_WIDTH), o_b.reshape(bs, SB_WIDTH), w_o16[l], n2, w_up16[l],
                      w_dn16[l], tm=bs, tf=1024)[None]
        ks_l.append(knew.reshape(bs, 1, SB_HEADS, SB_DH))
        vs_l.append(vnew.reshape(bs, 1, SB_HEADS, SB_DH))
        ss_l.append(s_new)
        cs_l.append(ns.reshape(bs, CONV_TAPS - 1, CONV_DIM))

    y_prompt = xp[:, N_META:t_real]
    return (y_prompt, xs.reshape(bs, 1, d_model),
            jnp.stack(kp_l), jnp.stack(vp_l), jnp.stack(sp_l), jnp.stack(cp_l),
            jnp.stack(ks_l), jnp.stack(vs_l), jnp.stack(ss_l), jnp.stack(cs_l))
```

```python
import functools
import math

import jax
import jax.numpy as jnp
from jax import lax
from jax.experimental import pallas as pl
from jax.experimental.pallas import tpu as pltpu

F32 = jnp.float32
BF16 = jnp.bfloat16
EPS = 1e-6
LOG2E = 1.4426950408889634

N_META = 16
GDN_HEADS = 4
GDN_D = 128
GDN_WIDTH = GDN_HEADS * GDN_D
CONV_DIM = 3 * GDN_WIDTH
CONV_TAPS = 4
GDN_CHUNK = 64
SB_HEADS = 8
SB_DH = 64
SB_WIDTH = SB_HEADS * SB_DH
AB_COLS = 128
C_CONV = 0
C_Z = C_CONV + CONV_DIM
C_SQ = C_Z + GDN_WIDTH
C_SK = C_SQ + SB_WIDTH
C_SV = C_SK + SB_WIDTH
C_AB = C_SV + SB_WIDTH
IN_COLS_PAD = C_AB + AB_COLS

BF16_ROWS = 16
ROW_TILE = 256
GDN_TILE = 2 * GDN_CHUNK
GDN_BATCH_TILE = 4
GDN_SAMPLE_TILE = 4
VMEM_LIMIT = 56 * 1024 * 1024


def _cparams(sem):
    return pltpu.CompilerParams(dimension_semantics=sem, vmem_limit_bytes=VMEM_LIMIT)


def _dot(a, b):
    return jnp.dot(a, b, preferred_element_type=F32)


def _dot_nt(a, b):
    return lax.dot_general(a, b, (((1,), (1,)), ((), ())), preferred_element_type=F32)


def _split2(x):
    hi = x.astype(BF16)
    lo = (x - hi.astype(F32)).astype(BF16)
    return hi, lo


def _split3(x):
    hi = x.astype(BF16)
    r = x - hi.astype(F32)
    mid = r.astype(BF16)
    lo = (r - mid.astype(F32)).astype(BF16)
    return hi, mid, lo


def _dot3(a, b):
    ah, al = _split2(a)
    bh, bl = _split2(b)
    return _dot(ah, bh) + (_dot(ah, bl) + _dot(al, bh))


def _sigmoid(x):
    return 1.0 / (1.0 + jnp.exp(-x))


def _softplus(x):
    return jnp.maximum(x, 0.0) + jnp.log(1.0 + jnp.exp(-jnp.abs(x)))


def _inproj_core(x_ref, nw_ref, w_ref, gseg_ref, qnw_ref, knw_ref):
    x = x_ref[0]
    ms = jnp.mean(x * x, axis=-1, keepdims=True)
    h = (x * lax.rsqrt(ms + EPS) * nw_ref[...]).astype(BF16)

    def proj(lo, hi):
        return _dot(h, w_ref[:, lo:hi])

    def headnorm(s, wrow):
        hi, lo = _split2(s * s)
        ss = _dot(hi, gseg_ref[...]) + _dot(lo, gseg_ref[...])
        return s * lax.rsqrt(ss * (1.0 / SB_DH) + EPS) * wrow

    qn = headnorm(proj(C_SQ, C_SK), qnw_ref[...]) * (-(SB_DH ** -0.5) * LOG2E)
    kn = headnorm(proj(C_SK, C_SV), knw_ref[...])
    return proj, qn, kn, proj(C_SV, C_AB)


def _inproj_sample_kernel(x_ref, nw_ref, w_ref, gseg_ref, qnw_ref, knw_ref,
                          conv_ref, z_ref, ab_ref, q_ref, knew_ref, vnew_ref):
    proj, qn, kn, sv = _inproj_core(x_ref, nw_ref, w_ref, gseg_ref, qnw_ref, knw_ref)
    conv_ref[0] = proj(C_CONV, C_Z)
    z_ref[0] = proj(C_Z, C_SQ)
    ab_ref[0] = proj(C_AB, IN_COLS_PAD)
    q_ref[0] = qn
    knew_ref[0] = kn
    vnew_ref[0] = sv


def _inproj_prompt_kernel(x_ref, nw_ref, w_ref, gseg_ref, qnw_ref, knw_ref, cw_ref, pa_ref, pb_ref,
                          z_ref, u_ref, gb_ref, gbt_ref, ctail_ref, q_ref, kt_ref, v_ref, knew_ref, vnew_ref,
                          xs_ref, hist_ref, *, t_real, tail_tile, tail_row):
    j = pl.program_id(1)
    tm = x_ref.shape[1]
    proj, qn, kn, sv = _inproj_core(x_ref, nw_ref, w_ref, gseg_ref, qnw_ref, knw_ref)
    for hd in range(SB_HEADS):
        q_ref[0, hd] = qn[:, hd * SB_DH:(hd + 1) * SB_DH].astype(BF16)
        v_ref[0, hd] = sv[:, hd * SB_DH:(hd + 1) * SB_DH].astype(BF16)
    knt = kn.T
    kt_ref[0] = knt.astype(BF16)
    knew_ref[0] = knt
    vnew_ref[0] = sv.T
    z_ref[0] = proj(C_Z, C_SQ)

    conv = proj(C_CONV, C_Z)
    xs_ref[0:8] = jnp.where(j > 0, hist_ref[...], 0.0)
    xs_ref[8:] = conv
    hist_ref[...] = conv[tm - 8:]
    out = xs_ref[pl.ds(5, tm)] * cw_ref[0:1]
    for tap in range(1, CONV_TAPS):
        out = out + xs_ref[pl.ds(5 + tap, tm)] * cw_ref[tap:tap + 1]
    row = j * tm + lax.broadcasted_iota(jnp.int32, (tm, AB_COLS), 0)
    u, gb = _gdn_inputs(out, proj(C_AB, IN_COLS_PAD), pa_ref[...], pb_ref[...], row < t_real)
    u_ref[0] = u
    gb_ref[0] = gb
    gbt_ref[0] = gb.T[0:8]

    @pl.when(j == tail_tile)
    def _():
        ctail_ref[0] = conv[tail_row:tail_row + 8]


def _inproj_specs(tm, d):
    const = lambda i, j: (0, 0)
    return [
        pl.BlockSpec((1, tm, d), lambda i, j: (i, j, 0)),
        pl.BlockSpec((1, d), const),
        pl.BlockSpec((d, IN_COLS_PAD), const),
        pl.BlockSpec((SB_WIDTH, SB_WIDTH), const),
        pl.BlockSpec((1, SB_WIDTH), const),
        pl.BlockSpec((1, SB_WIDTH), const),
    ]


def _inproj_sample(x, nw, w, gseg, qnw, knw):
    b, t, d = x.shape
    widths = [CONV_DIM, GDN_WIDTH, AB_COLS, SB_WIDTH, SB_WIDTH, SB_WIDTH]
    return pl.pallas_call(
        _inproj_sample_kernel,
        name="inproj_sample",
        grid=(b, 1),
        in_specs=_inproj_specs(t, d),
        out_specs=[pl.BlockSpec((1, t, n), lambda i, j: (i, j, 0)) for n in widths],
        out_shape=[jax.ShapeDtypeStruct((b, t, n), F32) for n in widths],
        compiler_params=_cparams(("parallel", "parallel")),
    )(x, nw, w, gseg, qnw, knw)


def _inproj_prompt(x, nw, w, gseg, qnw, knw, cw, pa, pb, *, tm, t_real):
    b, t, d = x.shape
    row = lambda i, j: (i, j, 0)
    const = lambda i, j: (0, 0)
    hm = lambda i, j: (i, 0, j, 0)
    first_tail = t_real - (CONV_TAPS - 1)
    tail_tile = first_tail // tm
    tail_row = (first_tail % tm) // 8 * 8
    tail_off = first_tail % tm - tail_row
    assert tail_off + CONV_TAPS - 1 <= 8, "conv-state rows must sit in one 8-row group"
    f32_rows = lambda n: (jax.ShapeDtypeStruct((b, t, n), F32), pl.BlockSpec((1, tm, n), row))
    new_kv = (jax.ShapeDtypeStruct((b, SB_WIDTH, t_real), F32),
              pl.BlockSpec((1, SB_WIDTH, tm), lambda i, j: (i, 0, j)))
    outs = [
        f32_rows(GDN_WIDTH), f32_rows(CONV_DIM), f32_rows(AB_COLS),
        (jax.ShapeDtypeStruct((b, 8, t), F32), pl.BlockSpec((1, 8, tm), lambda i, j: (i, 0, j))),
        (jax.ShapeDtypeStruct((b, 8, CONV_DIM), F32), pl.BlockSpec((1, 8, CONV_DIM), lambda i, j: (i, 0, 0))),
        (jax.ShapeDtypeStruct((b, SB_HEADS, t, SB_DH), BF16), pl.BlockSpec((1, SB_HEADS, tm, SB_DH), hm)),
        (jax.ShapeDtypeStruct((b, SB_WIDTH, t), BF16), pl.BlockSpec((1, SB_WIDTH, tm), lambda i, j: (i, 0, j))),
        (jax.ShapeDtypeStruct((b, SB_HEADS, t, SB_DH), BF16), pl.BlockSpec((1, SB_HEADS, tm, SB_DH), hm)),
        new_kv, new_kv,
    ]
    in_specs = _inproj_specs(tm, d) + [pl.BlockSpec((CONV_TAPS, CONV_DIM), const),
                                       pl.BlockSpec((1, AB_COLS), const), pl.BlockSpec((1, AB_COLS), const)]
    res = pl.pallas_call(
        functools.partial(_inproj_prompt_kernel, t_real=t_real, tail_tile=tail_tile, tail_row=tail_row),
        name="inproj_prompt",
        grid=(b, t // tm),
        in_specs=in_specs,
        out_specs=[o[1] for o in outs],
        out_shape=[o[0] for o in outs],
        scratch_shapes=[pltpu.VMEM((tm + 8, CONV_DIM), F32), pltpu.VMEM((8, CONV_DIM), F32)],
        compiler_params=_cparams(("parallel", "arbitrary")),
    )(x, nw, w, gseg, qnw, knw, cw, pa, pb)
    z, u, gb, gbt, ctail, q, kt, v, knew, vnew = res
    return z, u, gb, gbt, ctail[:, tail_off:tail_off + CONV_TAPS - 1], q, kt, v, knew, vnew


def _gdn_inputs(conv_out, ab, pa, pb, valid):
    u = conv_out * _sigmoid(conv_out)
    parts = []
    for hd in range(2 * GDN_HEADS):
        sl = u[:, hd * GDN_D:(hd + 1) * GDN_D]
        nrm = sl * lax.rsqrt(jnp.sum(sl * sl, axis=-1, keepdims=True) + EPS)
        parts.append(nrm * (GDN_D ** -0.5) if hd < GDN_HEADS else nrm)
    parts.append(u[:, 2 * GDN_WIDTH:])
    g = -jnp.exp(pa) * _softplus(ab + pb)
    beta = _sigmoid(ab)
    lane = lax.broadcasted_iota(jnp.int32, ab.shape, 1)
    gb = jnp.where(lane < GDN_HEADS, g, jnp.where(lane < 2 * GDN_HEADS, beta, 0.0))
    if valid is not None:
        gb = jnp.where(valid, gb, 0.0)
    return jnp.concatenate(parts, axis=1), gb


def _conv_sample_kernel(x_ref, st_ref, cw_ref, ab_ref, pa_ref, pb_ref, u_ref, gb_ref, ns_ref):
    c = x_ref.shape[1]
    taps = [st_ref[:, k * c:(k + 1) * c] for k in range(CONV_TAPS - 1)] + [x_ref[...]]
    out = taps[0] * cw_ref[0:1]
    for tap in range(1, CONV_TAPS):
        out = out + taps[tap] * cw_ref[tap:tap + 1]
    u, gb = _gdn_inputs(out, ab_ref[...], pa_ref[...], pb_ref[...], None)
    u_ref[...] = u
    gb_ref[...] = gb
    for k in range(CONV_TAPS - 1):
        ns_ref[:, k * c:(k + 1) * c] = taps[k + 1]


def _conv_sample(conv_in, state, cw, ab, pa, pb):
    n, c = conv_in.shape
    return pl.pallas_call(
        _conv_sample_kernel,
        name="conv_sample",
        out_shape=[
            jax.ShapeDtypeStruct((n, c), F32),
            jax.ShapeDtypeStruct((n, AB_COLS), F32),
            jax.ShapeDtypeStruct((n, (CONV_TAPS - 1) * c), F32),
        ],
        compiler_params=pltpu.CompilerParams(vmem_limit_bytes=VMEM_LIMIT),
    )(conv_in, state, cw, ab, pa, pb)


def _gated_norm(o, z, nw):
    return o * lax.rsqrt(jnp.mean(o * o, axis=-1, keepdims=True) + EPS) * nw * (z * _sigmoid(z))


def _gdn_chunk_kernel(u_ref, gb_ref, gbt_ref, z_ref, nw_ref, ltri_ref, ltrit_ref, o_ref, sout_ref, s_ref):
    step = pl.program_id(1)
    n = GDN_TILE
    c = GDN_CHUNK

    @pl.when(step == 0)
    def _():
        s_ref[...] = jnp.zeros_like(s_ref)

    ltri = ltri_ref[...]
    ltrit = ltrit_ref[...]
    batch = range(u_ref.shape[0])
    gb = [gb_ref[bi] for bi in batch]
    gparts = [_split3(gb[bi]) for bi in batch]
    tparts = [_split3(gbt_ref[bi]) for bi in batch]
    cum_cols = [_dot(ltri, p[0]) + (_dot(ltri, p[1]) + _dot(ltri, p[2])) for p in gparts]
    cum_rows = [_dot(p[0], ltrit) + (_dot(p[1], ltrit) + _dot(p[2], ltrit)) for p in tparts]

    row = lax.broadcasted_iota(jnp.int32, (n, n), 0)
    col = lax.broadcasted_iota(jnp.int32, (n, n), 1)
    same = (row >= c) == (col >= c)
    causal = same & (row >= col)
    strict = same & (row > col)
    eye = (row == col).astype(F32)
    second = lax.broadcasted_iota(jnp.int32, (n, 1), 0) >= c
    zeros_c = jnp.zeros((c, GDN_D), F32)

    chains = [(bi, hd) for bi in batch for hd in range(GDN_HEADS)]
    ids = range(len(chains))
    lanes = lambda hd, base: slice(base + hd * GDN_D, base + (hd + 1) * GDN_D)
    q = [u_ref[bi, :, lanes(hd, 0)] for bi, hd in chains]
    k = [u_ref[bi, :, lanes(hd, GDN_WIDTH)] for bi, hd in chains]
    v = [u_ref[bi, :, lanes(hd, 2 * GDN_WIDTH)] for bi, hd in chains]
    cc = [cum_cols[bi][:, hd:hd + 1] for bi, hd in chains]
    cr = [cum_rows[bi][hd:hd + 1, :] for bi, hd in chains]
    beta = [gb[bi][:, GDN_HEADS + hd:GDN_HEADS + hd + 1] for bi, hd in chains]
    gam = [jnp.where(causal, jnp.exp(jnp.where(causal, cc[x] - cr[x], 0.0)), 0.0) for x in ids]
    kb = [k[x] * beta[x] for x in ids]
    k16 = [k[x].astype(BF16) for x in ids]
    a_mat = [jnp.where(strict, _dot_nt(kb[x].astype(BF16), k16[x]) * gam[x], 0.0) for x in ids]
    qk = [(_dot_nt(q[x].astype(BF16), k16[x]) * gam[x]).astype(BF16) for x in ids]
    inv = [eye - a_mat[x] for x in ids]
    pw = [_dot3(a_mat[x], a_mat[x]) for x in ids]
    for _ in range(4):
        both = [_dot3(jnp.concatenate([inv[x], pw[x]], axis=0), pw[x]) for x in ids]
        inv = [inv[x] + both[x][:n] for x in ids]
        pw = [both[x][n:] for x in ids]
    inv = [inv[x] + _dot3(inv[x], pw[x]) for x in ids]
    ecum = [jnp.exp(cc[x]) for x in ids]
    uw = [_dot3(inv[x], jnp.concatenate([v[x] * beta[x], kb[x] * ecum[x]], axis=1)) for x in ids]
    uu = [uw[x][:, :GDN_D] for x in ids]
    ww = [uw[x][:, GDN_D:].astype(BF16) for x in ids]
    qg = [(q[x] * ecum[x]).astype(BF16) for x in ids]
    g_last0 = [cc[x][c - 1:c] for x in ids]
    g_last1 = [cc[x][n - 1:n] for x in ids]
    kdt = [(k[x] * jnp.exp(jnp.where(second, g_last1[x], g_last0[x]) - cc[x])).T.astype(BF16) for x in ids]

    s0 = [s_ref[bi, hd] for bi, hd in chains]
    s0b = [s0[x].astype(BF16) for x in ids]
    vn0 = [uu[x][:c] - _dot(ww[x][:c], s0b[x]) for x in ids]
    oi0 = [_dot(qg[x][:c], s0b[x]) for x in ids]
    s1 = [s0[x] * jnp.exp(g_last0[x]) + _dot(kdt[x], jnp.concatenate([vn0[x], zeros_c], axis=0).astype(BF16))
          for x in ids]
    s1b = [s1[x].astype(BF16) for x in ids]
    vn1 = [uu[x][c:] - _dot(ww[x][c:], s1b[x]) for x in ids]
    oi1 = [_dot(qg[x][c:], s1b[x]) for x in ids]
    s2 = [s1[x] * jnp.exp(g_last1[x]) + _dot(kdt[x], jnp.concatenate([zeros_c, vn1[x]], axis=0).astype(BF16))
          for x in ids]
    for x, (bi, hd) in enumerate(chains):
        s_ref[bi, hd] = s2[x]
        vn = jnp.concatenate([vn0[x], vn1[x]], axis=0).astype(BF16)
        o = jnp.concatenate([oi0[x], oi1[x]], axis=0) + _dot(qk[x], vn)
        o_ref[bi, :, lanes(hd, 0)] = _gated_norm(o, z_ref[bi, :, lanes(hd, 0)], nw_ref[...]).astype(BF16)

    @pl.when(step == pl.num_programs(1) - 1)
    def _():
        sout_ref[...] = s_ref[...]


def _gdn_prompt(u, gb, gbt, z, nw, ltri, ltrit):
    b, t, _ = u.shape
    n = GDN_TILE
    nb = math.gcd(b, GDN_BATCH_TILE)
    row = lambda i, j: (i, j, 0)
    const = lambda i, j: (0, 0)
    return pl.pallas_call(
        _gdn_chunk_kernel,
        name="gdn_prompt",
        grid=(b // nb, t // n),
        in_specs=[
            pl.BlockSpec((nb, n, CONV_DIM), row),
            pl.BlockSpec((nb, n, AB_COLS), row),
            pl.BlockSpec((nb, 8, n), lambda i, j: (i, 0, j)),
            pl.BlockSpec((nb, n, GDN_WIDTH), row),
            pl.BlockSpec((1, GDN_D), const),
            pl.BlockSpec((n, n), const),
            pl.BlockSpec((n, n), const),
        ],
        out_specs=[
            pl.BlockSpec((nb, n, GDN_WIDTH), row),
            pl.BlockSpec((nb, GDN_HEADS, GDN_D, GDN_D), lambda i, j: (i, 0, 0, 0)),
        ],
        out_shape=[
            jax.ShapeDtypeStruct((b, t, GDN_WIDTH), BF16),
            jax.ShapeDtypeStruct((b, GDN_HEADS, GDN_D, GDN_D), F32),
        ],
        scratch_shapes=[pltpu.VMEM((nb, GDN_HEADS, GDN_D, GDN_D), F32)],
        compiler_params=_cparams(("parallel", "arbitrary")),
    )(u, gb, gbt, z, nw, ltri, ltrit)


def _gdn_step_kernel(s_ref, u_ref, gb_ref, z_ref, nw_ref, *rest):
    sout_ref, o_ref = rest[-2:]
    row = lax.broadcasted_iota(jnp.int32, (GDN_D, GDN_D), 0)
    col = lax.broadcasted_iota(jnp.int32, (GDN_D, GDN_D), 1)
    eye = row == col

    def column(r):
        return jnp.sum(jnp.where(eye, r, 0.0), axis=1, keepdims=True)

    for bi in range(s_ref.shape[0]):
        for hd in range(GDN_HEADS):
            q = u_ref[bi, :, hd * GDN_D:(hd + 1) * GDN_D]
            k = u_ref[bi, :, GDN_WIDTH + hd * GDN_D:GDN_WIDTH + (hd + 1) * GDN_D]
            v = u_ref[bi, :, 2 * GDN_WIDTH + hd * GDN_D:2 * GDN_WIDTH + (hd + 1) * GDN_D]
            g = gb_ref[bi, :, hd:hd + 1]
            beta = gb_ref[bi, :, GDN_HEADS + hd:GDN_HEADS + hd + 1]
            kc = column(k)
            s = s_ref[bi, hd] * jnp.exp(g)
            err = v - jnp.sum(kc * s, axis=0, keepdims=True)
            s = s + kc * (err * beta)
            sout_ref[0, bi, hd] = s
            o = jnp.sum(column(q) * s, axis=0, keepdims=True)
            zg = z_ref[bi, :, hd * GDN_D:(hd + 1) * GDN_D]
            o_ref[bi, :, hd * GDN_D:(hd + 1) * GDN_D] = _gated_norm(o, zg, nw_ref[...]).astype(BF16)


def _gdn_sample(state, u, gb, z, nw, stacked, *, layer, depth):
    n = state.shape[0]
    bt = GDN_SAMPLE_TILE
    row = lambda i: (i, 0, 0)
    st = lambda i: (i, 0, 0, 0)
    in_specs = [
        pl.BlockSpec((bt, GDN_HEADS, GDN_D, GDN_D), st),
        pl.BlockSpec((bt, 1, CONV_DIM), row),
        pl.BlockSpec((bt, 1, AB_COLS), row),
        pl.BlockSpec((bt, 1, GDN_WIDTH), row),
        pl.BlockSpec((1, GDN_D), lambda i: (0, 0)),
    ]
    args = [state, u.reshape(n, 1, CONV_DIM), gb.reshape(n, 1, AB_COLS), z.reshape(n, 1, GDN_WIDTH), nw]
    aliases = {}
    if stacked is not None:
        aliases = {len(args): 0}
        in_specs.append(pl.BlockSpec(memory_space=pl.ANY))
        args.append(stacked)
    return pl.pallas_call(
        _gdn_step_kernel,
        name="gdn_sample",
        grid=(n // bt,),
        in_specs=in_specs,
        out_specs=[
            pl.BlockSpec((1, bt, GDN_HEADS, GDN_D, GDN_D), lambda i: (layer, i, 0, 0, 0)),
            pl.BlockSpec((bt, 1, GDN_WIDTH), row),
        ],
        out_shape=[
            jax.ShapeDtypeStruct((depth,) + state.shape, F32),
            jax.ShapeDtypeStruct((n, 1, GDN_WIDTH), BF16),
        ],
        input_output_aliases=aliases,
        compiler_params=_cparams(("parallel",)),
    )(*args)


def _neg_abs(x):
    bits = lax.bitcast_convert_type(x, jnp.uint32) | jnp.uint32(0x80000000)
    return lax.bitcast_convert_type(bits, F32)


def _sb_blocks(zns, later, carries, mask):
    stays = [[jnp.minimum(zn, 0.0) - jnp.log2(1.0 + jnp.exp2(_neg_abs(zn))) for zn in ch] for ch in zns]
    if mask is not None:
        stays = [[jnp.where(mask, s, 0.0) for s in ch] for ch in stays]
    scans = [[_dot(s.astype(BF16), later) for s in ch] for ch in stays]
    sums = [[jnp.sum(s, axis=1, keepdims=True) for s in ch] for ch in stays]
    ws, out_carries = [], []
    for c, carry in enumerate(carries):
        ws.append([])
        for zn, stay, scan, rs in zip(zns[c], stays[c], scans[c], sums[c]):
            w = jnp.exp2((stay - zn) + (scan + carry))
            if mask is not None:
                w = jnp.where(mask, w, 0.0)
            ws[c].append(w.astype(BF16))
            carry = carry + rs
        out_carries.append(carry)
    return ws, out_carries


def _sb_prompt_kernel(bias_ref, q_ref, kt_ref, v_ref, later_ref, o_ref, zn_ref, w_ref, acc_ref, car_ref, *,
                      last_rows):
    hp = pl.program_id(1)
    i = pl.program_id(2)
    nq = q_ref.shape[2]
    heads = range(2)

    def span(blk):
        return pl.ds(pl.multiple_of(blk * nq, nq), nq)

    def run(nr):
        rows = slice(0, nr)
        diag_mask = lax.broadcasted_iota(jnp.int32, (nr, nq), 1) < lax.broadcasted_iota(jnp.int32, (nr, nq), 0)

        def logits(blk, slot):
            for hh in heads:
                zn_ref[slot, hh, rows] = _dot(q_ref[0, hh, rows], kt_ref[0, hh, :, span(blk)]) + bias_ref[2 * hp + hh]

        def weights(slot, mask):
            ws, carries = _sb_blocks([[zn_ref[slot, hh, rows]] for hh in heads], later_ref[...],
                                     [car_ref[hh, rows] for hh in heads], mask)
            for hh in heads:
                w_ref[hh, rows] = ws[hh][0]
                car_ref[hh, rows] = carries[hh]

        def attend(blk):
            for hh in heads:
                acc_ref[hh, rows] += _dot(w_ref[hh, rows], v_ref[0, hh, span(blk), :])

        def step(s, slot):
            attend(i - s + 1)
            logits(jnp.maximum(i - s - 1, 0), 1 - slot)
            weights(slot, None)

        acc_ref[...] = jnp.zeros_like(acc_ref)
        car_ref[...] = jnp.zeros_like(car_ref)
        logits(i, 0)
        logits(jnp.maximum(i - 1, 0), 1)
        weights(0, diag_mask)

        def body(p, _):
            step(2 * p + 1, 1)
            step(2 * p + 2, 0)
            return 0

        lax.fori_loop(0, i // 2, body, 0)

        @pl.when(i % 2 == 1)
        def _():
            step(i, 1)

        attend(0)
        o_ref[0] = jnp.concatenate([acc_ref[hh] for hh in heads], axis=1).astype(BF16)

    if last_rows == nq:
        run(nq)
    else:
        last = pl.num_programs(2) - 1
        pl.when(i < last)(lambda: run(nq))
        pl.when(i == last)(lambda: run(last_rows))


def _sb_prompt(q, kt, v, bias, later, *, tq, t_real):
    b, nh, t, dh = q.shape
    kt = kt.reshape(b, nh, dh, t)
    last_rows = min(tq, -(-(t_real - (t // tq - 1) * tq) // BF16_ROWS) * BF16_ROWS)
    return pl.pallas_call(
        functools.partial(_sb_prompt_kernel, last_rows=last_rows),
        name="sb_prompt",
        grid_spec=pltpu.PrefetchScalarGridSpec(
            num_scalar_prefetch=0,
            grid=(b, nh // 2, t // tq),
            in_specs=[
                pl.BlockSpec(memory_space=pltpu.SMEM),
                pl.BlockSpec((1, 2, tq, dh), lambda i, h, j: (i, h, j, 0)),
                pl.BlockSpec((1, 2, dh, t), lambda i, h, j: (i, h, 0, 0)),
                pl.BlockSpec((1, 2, t, dh), lambda i, h, j: (i, h, 0, 0)),
                pl.BlockSpec((tq, tq), lambda i, h, j: (0, 0)),
            ],
            out_specs=pl.BlockSpec((1, tq, 2 * dh), lambda i, h, j: (i, j, h)),
            scratch_shapes=[pltpu.VMEM((2, 2, tq, tq), F32), pltpu.VMEM((2, tq, tq), BF16),
                            pltpu.VMEM((2, tq, dh), F32), pltpu.VMEM((2, tq, 1), F32)],
        ),
        out_shape=jax.ShapeDtypeStruct((b, t, nh * dh), BF16),
        compiler_params=_cparams(("parallel", "parallel", "arbitrary")),
    )(bias, q, kt, v, later)


def _sb_decode_kernel(pt_ref, q_ref, bias_ref, later_ref, *refs, n_pages):
    k_refs = refs[:n_pages]
    v_refs = refs[n_pages:2 * n_pages]
    o_ref = refs[2 * n_pages]
    head_of_lane = lax.broadcasted_iota(jnp.int32, (SB_HEADS, SB_WIDTH), 1) // SB_DH
    own = head_of_lane == lax.broadcasted_iota(jnp.int32, (SB_HEADS, SB_WIDTH), 0)
    qbd = jnp.where(own, q_ref[0], 0.0).astype(BF16)
    order = range(n_pages - 1, -1, -1)
    zs = [_dot(qbd, k_refs[p][0].astype(BF16)) + bias_ref[...] for p in order]
    ws, _ = _sb_blocks([zs], later_ref[...], [jnp.zeros((SB_HEADS, 1), F32)], None)
    acc = _dot_nt(ws[0][0], v_refs[order[0]][0].astype(BF16))
    for w, p in zip(ws[0][1:], order[1:]):
        acc = acc + _dot_nt(w, v_refs[p][0].astype(BF16))
    o_ref[0] = jnp.sum(jnp.where(own, acc, 0.0), axis=0, keepdims=True).astype(BF16)


def _sb_decode(q, cache_k, cache_v, page_table, bias_b, later, *, layer):
    n, n_pages = page_table.shape
    depth, n_pool, page, nh, dh = cache_k.shape
    ck = jnp.transpose(cache_k, (0, 1, 3, 4, 2)).reshape(depth * n_pool, nh * dh, page)
    cv = jnp.transpose(cache_v, (0, 1, 3, 4, 2)).reshape(depth * n_pool, nh * dh, page)

    def page_spec(p):
        return pl.BlockSpec((1, nh * dh, page), lambda i, pt: (layer * n_pool + pt[i * n_pages + p], 0, 0))

    row = lambda i, pt: (i, 0, 0)
    return pl.pallas_call(
        functools.partial(_sb_decode_kernel, n_pages=n_pages),
        name="sb_decode",
        grid_spec=pltpu.PrefetchScalarGridSpec(
            num_scalar_prefetch=1,
            grid=(n,),
            in_specs=[
                pl.BlockSpec((1, 1, nh * dh), row),
                pl.BlockSpec((nh, 128), lambda i, pt: (0, 0)),
                pl.BlockSpec((page, page), lambda i, pt: (0, 0)),
            ] + [page_spec(p) for p in range(n_pages)] * 2,
            out_specs=pl.BlockSpec((1, 1, nh * dh), row),
        ),
        out_shape=jax.ShapeDtypeStruct((n, 1, nh * dh), BF16),
        compiler_params=_cparams(("parallel",)),
    )(page_table.reshape(-1), q.reshape(n, 1, nh * dh), bias_b, later, *([ck] * n_pages), *([cv] * n_pages))


def _mlp_kernel(x_ref, oa_ref, ob_ref, wo_ref, nw_ref, wup_ref, wdn_ref, y_ref, x1_ref, xn_ref, acc_ref):
    j = pl.program_id(1)

    @pl.when(j == 0)
    def _():
        x1 = x_ref[...] + (_dot(oa_ref[...], wo_ref[:GDN_WIDTH]) + _dot(ob_ref[...], wo_ref[GDN_WIDTH:]))
        x1_ref[...] = x1
        ms = jnp.mean(x1 * x1, axis=-1, keepdims=True)
        xn_ref[...] = (x1 * lax.rsqrt(ms + EPS) * nw_ref[...]).astype(BF16)
        acc_ref[...] = jnp.zeros_like(acc_ref)

    h = jnp.maximum(_dot(xn_ref[...], wup_ref[...]), 0.0)
    acc_ref[...] += _dot((h * h).astype(BF16), wdn_ref[...])

    @pl.when(j == pl.num_programs(1) - 1)
    def _():
        y_ref[...] = x1_ref[...] + acc_ref[...]


def _mix_mlp(x, oa, ob, wo, nw, wup, wdn, *, tm, tf):
    m, d = x.shape
    ff = wup.shape[1]
    row = lambda i, j: (i, 0)
    const = lambda i, j: (0, 0)
    return pl.pallas_call(
        _mlp_kernel,
        name="mix_mlp",
        grid=(m // tm, ff // tf),
        in_specs=[
            pl.BlockSpec((tm, d), row),
            pl.BlockSpec((tm, GDN_WIDTH), row),
            pl.BlockSpec((tm, SB_WIDTH), row),
            pl.BlockSpec((d, d), const),
            pl.BlockSpec((1, d), const),
            pl.BlockSpec((d, tf), lambda i, j: (0, j)),
            pl.BlockSpec((tf, d), lambda i, j: (j, 0)),
        ],
        out_specs=pl.BlockSpec((tm, d), row),
        out_shape=jax.ShapeDtypeStruct((m, d), F32),
        scratch_shapes=[pltpu.VMEM((tm, d), F32), pltpu.VMEM((tm, d), BF16), pltpu.VMEM((tm, d), F32)],
        compiler_params=_cparams(("parallel", "arbitrary")),
    )(x, oa, ob, wo, nw, wup, wdn)


def _tri_consts(n_chunk_tile, chunk, tq, page):
    r = jnp.arange(n_chunk_tile)
    same = (r[:, None] // chunk) == (r[None, :] // chunk)
    ltri = (same & (r[None, :] <= r[:, None])).astype(BF16)

    def later(nk):
        k = jnp.arange(nk)
        return (k[:, None] > k[None, :]).astype(BF16)

    return ltri, ltri.T, later(tq), later(page)


def kernel(x_prompt, x_sample, cache_k, cache_v, page_table, state_gdn, state_conv, meta_tokens, norm1_w, w_in,
           conv_w, a_log, dt_bias, gdn_norm_w, q_norm_w, k_norm_w, sb_bias, w_o, norm2_w, w_up, w_down):
    depth, d_model = norm1_w.shape
    bp, seq, _ = x_prompt.shape
    bs = x_sample.shape[0]
    page = cache_k.shape[2]
    t_real = N_META + seq
    tp = -(-t_real // ROW_TILE) * ROW_TILE

    c0 = CONV_DIM
    c1 = c0 + GDN_WIDTH
    c2 = c1 + GDN_HEADS
    c3 = c2 + GDN_HEADS
    w_in_r = jnp.concatenate(
        [w_in[..., :c1], w_in[..., c3:], w_in[..., c1:c3],
         jnp.zeros((depth, d_model, AB_COLS - 2 * GDN_HEADS), w_in.dtype)], axis=-1).astype(BF16)
    w_o16, w_up16, w_dn16 = w_o.astype(BF16), w_up.astype(BF16), w_down.astype(BF16)
    pad_lanes = lambda a: jnp.pad(a, ((0, 0), (0, AB_COLS - a.shape[1])))[:, None, :]
    pa, pb = pad_lanes(a_log), pad_lanes(dt_bias)
    qnw = jnp.tile(q_norm_w, (1, SB_HEADS))[:, None, :]
    knw = jnp.tile(k_norm_w, (1, SB_HEADS))[:, None, :]
    lane_head = jnp.arange(SB_WIDTH) // SB_DH
    gseg = (lane_head[:, None] == lane_head[None, :]).astype(BF16)
    ltri, ltrit, later_q, later_p = _tri_consts(GDN_TILE, GDN_CHUNK, ROW_TILE, page)
    sb_bias2 = -sb_bias * LOG2E
    bias_b = jnp.broadcast_to(sb_bias2[:, :, None], (depth, SB_HEADS, 128))

    meta = jnp.broadcast_to(meta_tokens[None], (bp, N_META, d_model))
    xp = jnp.concatenate([meta, x_prompt, jnp.zeros((bp, tp - t_real, d_model), x_prompt.dtype)], axis=1)
    xs = x_sample.reshape(1, bs, d_model)
    st_conv = state_conv.reshape(depth, bs, (CONV_TAPS - 1) * CONV_DIM)

    kp_l, vp_l, sp_l, cp_l = [], [], [], []
    ks_l, vs_l, cs_l = [], [], []
    ss_all = None
    for l in range(depth):
        n1, n2 = norm1_w[l][None], norm2_w[l][None]
        gnw = gdn_norm_w[l][None]
        z, u, gb, gbt, ctail, q, kt, v, knew, vnew = _inproj_prompt(
            xp, n1, w_in_r[l], gseg, qnw[l], knw[l], conv_w[l], pa[l], pb[l], tm=ROW_TILE, t_real=t_real)
        o_a, s_fin = _gdn_prompt(u, gb, gbt, z, gnw, ltri, ltrit)
        o_b = _sb_prompt(q, kt, v, sb_bias2[l], later_q, tq=ROW_TILE, t_real=t_real)
        xp = _mix_mlp(xp.reshape(bp * tp, d_model), o_a.reshape(bp * tp, GDN_WIDTH), o_b.reshape(bp * tp, SB_WIDTH),
                      w_o16[l], n2, w_up16[l], w_dn16[l], tm=1024, tf=1024).reshape(bp, tp, d_model)
        kp_l.append(knew.reshape(bp, SB_HEADS, SB_DH, t_real).transpose(0, 3, 1, 2))
        vp_l.append(vnew.reshape(bp, SB_HEADS, SB_DH, t_real).transpose(0, 3, 1, 2))
        sp_l.append(s_fin)
        cp_l.append(ctail)

        conv_in, z, ab, q, knew, vnew = _inproj_sample(xs, n1, w_in_r[l], gseg, qnw[l], knw[l])
        u, gb, ns = _conv_sample(conv_in[0], st_conv[l], conv_w[l], ab[0], pa[l], pb[l])
        ss_all, o_a = _gdn_sample(state_gdn[l], u, gb, z[0], gnw, ss_all, layer=l, depth=depth)
        o_b = _sb_decode(q[0], cache_k, cache_v, page_table, bias_b[l], later_p, layer=l)
        xs = _mix_mlp(xs[0], o_a.reshape(bs, GDN_WIDTH), o_b.reshape(bs, SB_WIDTH), w_o16[l], n2, w_up16[l],
                      w_dn16[l], tm=bs, tf=1024)[None]
        ks_l.append(knew.reshape(bs, 1, SB_HEADS, SB_DH))
        vs_l.append(vnew.reshape(bs, 1, SB_HEADS, SB_DH))
        cs_l.append(ns.reshape(bs, CONV_TAPS - 1, CONV_DIM))

    y_prompt = xp[:, N_META:t_real]
    return (y_prompt, xs.reshape(bs, 1, d_model),
            jnp.stack(kp_l), jnp.stack(vp_l), jnp.stack(sp_l), jnp.stack(cp_l),
            jnp.stack(ks_l), jnp.stack(vs_l), ss_all, jnp.stack(cs_l))
```

```python
import functools
import math

import jax
import jax.numpy as jnp
from jax import lax
from jax.experimental import pallas as pl
from jax.experimental.pallas import tpu as pltpu

F32 = jnp.float32
BF16 = jnp.bfloat16
EPS = 1e-6
LOG2E = 1.4426950408889634

N_META = 16
GDN_HEADS = 4
GDN_D = 128
GDN_WIDTH = GDN_HEADS * GDN_D
CONV_DIM = 3 * GDN_WIDTH
CONV_TAPS = 4
GDN_CHUNK = 64
SB_HEADS = 8
SB_DH = 64
SB_WIDTH = SB_HEADS * SB_DH
AB_COLS = 128
C_CONV = 0
C_Z = C_CONV + CONV_DIM
C_SQ = C_Z + GDN_WIDTH
C_SK = C_SQ + SB_WIDTH
C_SV = C_SK + SB_WIDTH
C_AB = C_SV + SB_WIDTH
IN_COLS_PAD = C_AB + AB_COLS

BF16_ROWS = 16
ROW_TILE = 256
GDN_TILE = 2 * GDN_CHUNK
SB_GROUP = 4
GDN_BATCH_TILE = 4
GDN_SAMPLE_TILE = 8
VMEM_LIMIT = 56 * 1024 * 1024


def _cparams(sem):
    return pltpu.CompilerParams(dimension_semantics=sem, vmem_limit_bytes=VMEM_LIMIT)


def _dot(a, b):
    return jnp.dot(a, b, preferred_element_type=F32)


def _dot_nt(a, b):
    return lax.dot_general(a, b, (((1,), (1,)), ((), ())), preferred_element_type=F32)


def _split2(x):
    hi = x.astype(BF16)
    lo = (x - hi.astype(F32)).astype(BF16)
    return hi, lo


def _split3(x):
    hi = x.astype(BF16)
    r = x - hi.astype(F32)
    mid = r.astype(BF16)
    lo = (r - mid.astype(F32)).astype(BF16)
    return hi, mid, lo


def _dot3(a, b):
    ah, al = _split2(a)
    bh, bl = _split2(b)
    return _dot(ah, bh) + (_dot(ah, bl) + _dot(al, bh))


def _sigmoid(x):
    return 1.0 / (1.0 + jnp.exp(-x))


def _softplus(x):
    return jnp.maximum(x, 0.0) + jnp.log(1.0 + jnp.exp(-jnp.abs(x)))


def _inproj_core(x_ref, nw_ref, w_ref, gseg_ref, qnw_ref, knw_ref):
    x = x_ref[0]
    ms = jnp.mean(x * x, axis=-1, keepdims=True)
    h = (x * lax.rsqrt(ms + EPS) * nw_ref[...]).astype(BF16)

    def proj(lo, hi):
        return _dot(h, w_ref[:, lo:hi])

    def headnorm(s, wrow):
        hi, lo = _split2(s * s)
        ss = _dot(hi, gseg_ref[...]) + _dot(lo, gseg_ref[...])
        return s * lax.rsqrt(ss * (1.0 / SB_DH) + EPS) * wrow

    qn = headnorm(proj(C_SQ, C_SK), qnw_ref[...]) * (-(SB_DH ** -0.5) * LOG2E)
    kn = headnorm(proj(C_SK, C_SV), knw_ref[...])
    return proj, qn, kn, proj(C_SV, C_AB)


def _inproj_sample_kernel(x_ref, nw_ref, w_ref, gseg_ref, qnw_ref, knw_ref,
                          conv_ref, z_ref, ab_ref, q_ref, knew_ref, vnew_ref):
    proj, qn, kn, sv = _inproj_core(x_ref, nw_ref, w_ref, gseg_ref, qnw_ref, knw_ref)
    conv_ref[0] = proj(C_CONV, C_Z)
    z_ref[0] = proj(C_Z, C_SQ)
    ab_ref[0] = proj(C_AB, IN_COLS_PAD)
    q_ref[0] = qn
    knew_ref[0] = kn
    vnew_ref[0] = sv


def _inproj_prompt_kernel(x_ref, nw_ref, w_ref, gseg_ref, qnw_ref, knw_ref, cw_ref, pa_ref, pb_ref,
                          z_ref, u_ref, gb_ref, gbt_ref, ctail_ref, q_ref, kt_ref, v_ref, knew_ref, vnew_ref,
                          xs_ref, hist_ref, *, t_real, tail_tile, tail_row):
    j = pl.program_id(1)
    tm = x_ref.shape[1]
    proj, qn, kn, sv = _inproj_core(x_ref, nw_ref, w_ref, gseg_ref, qnw_ref, knw_ref)
    for hd in range(SB_HEADS):
        q_ref[0, hd] = qn[:, hd * SB_DH:(hd + 1) * SB_DH].astype(BF16)
        v_ref[0, hd] = sv[:, hd * SB_DH:(hd + 1) * SB_DH].astype(BF16)
    knt = kn.T
    kt_ref[0] = knt.astype(BF16)
    knew_ref[0] = knt
    vnew_ref[0] = sv.T
    z_ref[0] = proj(C_Z, C_SQ)

    conv = proj(C_CONV, C_Z)
    xs_ref[0:8] = jnp.where(j > 0, hist_ref[...], 0.0)
    xs_ref[8:] = conv
    hist_ref[...] = conv[tm - 8:]
    out = xs_ref[pl.ds(5, tm)] * cw_ref[0:1]
    for tap in range(1, CONV_TAPS):
        out = out + xs_ref[pl.ds(5 + tap, tm)] * cw_ref[tap:tap + 1]
    row = j * tm + lax.broadcasted_iota(jnp.int32, (tm, AB_COLS), 0)
    u, gb = _gdn_inputs(out, proj(C_AB, IN_COLS_PAD), pa_ref[...], pb_ref[...], row < t_real)
    u_ref[0] = u
    gb_ref[0] = gb
    gbt_ref[0] = gb.T[0:8]

    @pl.when(j == tail_tile)
    def _():
        ctail_ref[0] = conv[tail_row:tail_row + 8]


def _inproj_specs(tm, d):
    const = lambda i, j: (0, 0)
    return [
        pl.BlockSpec((1, tm, d), lambda i, j: (i, j, 0)),
        pl.BlockSpec((1, d), const),
        pl.BlockSpec((d, IN_COLS_PAD), const),
        pl.BlockSpec((SB_WIDTH, SB_WIDTH), const),
        pl.BlockSpec((1, SB_WIDTH), const),
        pl.BlockSpec((1, SB_WIDTH), const),
    ]


def _inproj_sample(x, nw, w, gseg, qnw, knw):
    b, t, d = x.shape
    widths = [CONV_DIM, GDN_WIDTH, AB_COLS, SB_WIDTH, SB_WIDTH, SB_WIDTH]
    return pl.pallas_call(
        _inproj_sample_kernel,
        name="inproj_sample",
        grid=(b, 1),
        in_specs=_inproj_specs(t, d),
        out_specs=[pl.BlockSpec((1, t, n), lambda i, j: (i, j, 0)) for n in widths],
        out_shape=[jax.ShapeDtypeStruct((b, t, n), F32) for n in widths],
        compiler_params=_cparams(("parallel", "parallel")),
    )(x, nw, w, gseg, qnw, knw)


def _inproj_prompt(x, nw, w, gseg, qnw, knw, cw, pa, pb, *, tm, t_real):
    b, t, d = x.shape
    row = lambda i, j: (i, j, 0)
    const = lambda i, j: (0, 0)
    hm = lambda i, j: (i, 0, j, 0)
    first_tail = t_real - (CONV_TAPS - 1)
    tail_tile = first_tail // tm
    tail_row = (first_tail % tm) // 8 * 8
    tail_off = first_tail % tm - tail_row
    assert tail_off + CONV_TAPS - 1 <= 8, "conv-state rows must sit in one 8-row group"
    f32_rows = lambda n: (jax.ShapeDtypeStruct((b, t, n), F32), pl.BlockSpec((1, tm, n), row))
    new_kv = (jax.ShapeDtypeStruct((b, SB_WIDTH, t_real), F32),
              pl.BlockSpec((1, SB_WIDTH, tm), lambda i, j: (i, 0, j)))
    outs = [
        f32_rows(GDN_WIDTH), f32_rows(CONV_DIM), f32_rows(AB_COLS),
        (jax.ShapeDtypeStruct((b, 8, t), F32), pl.BlockSpec((1, 8, tm), lambda i, j: (i, 0, j))),
        (jax.ShapeDtypeStruct((b, 8, CONV_DIM), F32), pl.BlockSpec((1, 8, CONV_DIM), lambda i, j: (i, 0, 0))),
        (jax.ShapeDtypeStruct((b, SB_HEADS, t, SB_DH), BF16), pl.BlockSpec((1, SB_HEADS, tm, SB_DH), hm)),
        (jax.ShapeDtypeStruct((b, SB_WIDTH, t), BF16), pl.BlockSpec((1, SB_WIDTH, tm), lambda i, j: (i, 0, j))),
        (jax.ShapeDtypeStruct((b, SB_HEADS, t, SB_DH), BF16), pl.BlockSpec((1, SB_HEADS, tm, SB_DH), hm)),
        new_kv, new_kv,
    ]
    in_specs = _inproj_specs(tm, d) + [pl.BlockSpec((CONV_TAPS, CONV_DIM), const),
                                       pl.BlockSpec((1, AB_COLS), const), pl.BlockSpec((1, AB_COLS), const)]
    res = pl.pallas_call(
        functools.partial(_inproj_prompt_kernel, t_real=t_real, tail_tile=tail_tile, tail_row=tail_row),
        name="inproj_prompt",
        grid=(b, t // tm),
        in_specs=in_specs,
        out_specs=[o[1] for o in outs],
        out_shape=[o[0] for o in outs],
        scratch_shapes=[pltpu.VMEM((tm + 8, CONV_DIM), F32), pltpu.VMEM((8, CONV_DIM), F32)],
        compiler_params=_cparams(("parallel", "arbitrary")),
    )(x, nw, w, gseg, qnw, knw, cw, pa, pb)
    z, u, gb, gbt, ctail, q, kt, v, knew, vnew = res
    return z, u, gb, gbt, ctail[:, tail_off:tail_off + CONV_TAPS - 1], q, kt, v, knew, vnew


def _gdn_inputs(conv_out, ab, pa, pb, valid):
    u = conv_out * _sigmoid(conv_out)
    parts = []
    for hd in range(2 * GDN_HEADS):
        sl = u[:, hd * GDN_D:(hd + 1) * GDN_D]
        nrm = sl * lax.rsqrt(jnp.sum(sl * sl, axis=-1, keepdims=True) + EPS)
        parts.append(nrm * (GDN_D ** -0.5) if hd < GDN_HEADS else nrm)
    parts.append(u[:, 2 * GDN_WIDTH:])
    g = -jnp.exp(pa) * _softplus(ab + pb)
    beta = _sigmoid(ab)
    lane = lax.broadcasted_iota(jnp.int32, ab.shape, 1)
    gb = jnp.where(lane < GDN_HEADS, g, jnp.where(lane < 2 * GDN_HEADS, beta, 0.0))
    if valid is not None:
        gb = jnp.where(valid, gb, 0.0)
    return jnp.concatenate(parts, axis=1), gb


def _conv_sample_kernel(x_ref, st_ref, cw_ref, ab_ref, pa_ref, pb_ref, u_ref, gb_ref, ns_ref):
    c = x_ref.shape[1]
    taps = [st_ref[:, k * c:(k + 1) * c] for k in range(CONV_TAPS - 1)] + [x_ref[...]]
    out = taps[0] * cw_ref[0:1]
    for tap in range(1, CONV_TAPS):
        out = out + taps[tap] * cw_ref[tap:tap + 1]
    u, gb = _gdn_inputs(out, ab_ref[...], pa_ref[...], pb_ref[...], None)
    u_ref[...] = u
    gb_ref[...] = gb
    for k in range(CONV_TAPS - 1):
        ns_ref[:, k * c:(k + 1) * c] = taps[k + 1]


def _conv_sample(conv_in, state, cw, ab, pa, pb):
    n, c = conv_in.shape
    return pl.pallas_call(
        _conv_sample_kernel,
        name="conv_sample",
        out_shape=[
            jax.ShapeDtypeStruct((n, c), F32),
            jax.ShapeDtypeStruct((n, AB_COLS), F32),
            jax.ShapeDtypeStruct((n, (CONV_TAPS - 1) * c), F32),
        ],
        compiler_params=pltpu.CompilerParams(vmem_limit_bytes=VMEM_LIMIT),
    )(conv_in, state, cw, ab, pa, pb)


def _gated_norm(o, z, nw):
    return o * lax.rsqrt(jnp.mean(o * o, axis=-1, keepdims=True) + EPS) * nw * (z * _sigmoid(z))


def _gdn_chunk_kernel(u_ref, gb_ref, gbt_ref, z_ref, nw_ref, ltri_ref, ltrit_ref, o_ref, sout_ref, s_ref):
    step = pl.program_id(1)
    n = GDN_TILE
    c = GDN_CHUNK

    @pl.when(step == 0)
    def _():
        s_ref[...] = jnp.zeros_like(s_ref)

    ltri = ltri_ref[...]
    ltrit = ltrit_ref[...]
    batch = range(u_ref.shape[0])
    gb = [gb_ref[bi] for bi in batch]
    gparts = [_split3(gb[bi]) for bi in batch]
    tparts = [_split3(gbt_ref[bi]) for bi in batch]
    cum_cols = [_dot(ltri, p[0]) + (_dot(ltri, p[1]) + _dot(ltri, p[2])) for p in gparts]
    cum_rows = [_dot(p[0], ltrit) + (_dot(p[1], ltrit) + _dot(p[2], ltrit)) for p in tparts]

    row = lax.broadcasted_iota(jnp.int32, (n, n), 0)
    col = lax.broadcasted_iota(jnp.int32, (n, n), 1)
    same = (row >= c) == (col >= c)
    causal = same & (row >= col)
    strict = same & (row > col)
    eye = (row == col).astype(F32)
    second = lax.broadcasted_iota(jnp.int32, (n, 1), 0) >= c
    zeros_c = jnp.zeros((c, GDN_D), F32)

    chains = [(bi, hd) for bi in batch for hd in range(GDN_HEADS)]
    ids = range(len(chains))
    lanes = lambda hd, base: slice(base + hd * GDN_D, base + (hd + 1) * GDN_D)
    q = [u_ref[bi, :, lanes(hd, 0)] for bi, hd in chains]
    k = [u_ref[bi, :, lanes(hd, GDN_WIDTH)] for bi, hd in chains]
    v = [u_ref[bi, :, lanes(hd, 2 * GDN_WIDTH)] for bi, hd in chains]
    cc = [cum_cols[bi][:, hd:hd + 1] for bi, hd in chains]
    cr = [cum_rows[bi][hd:hd + 1, :] for bi, hd in chains]
    beta = [gb[bi][:, GDN_HEADS + hd:GDN_HEADS + hd + 1] for bi, hd in chains]
    gam = [jnp.where(causal, jnp.exp(jnp.where(causal, cc[x] - cr[x], 0.0)), 0.0) for x in ids]
    kb = [k[x] * beta[x] for x in ids]
    k16 = [k[x].astype(BF16) for x in ids]
    a_mat = [jnp.where(strict, _dot_nt(kb[x].astype(BF16), k16[x]) * gam[x], 0.0) for x in ids]
    qk = [(_dot_nt(q[x].astype(BF16), k16[x]) * gam[x]).astype(BF16) for x in ids]
    inv = [eye - a_mat[x] for x in ids]
    pw = [_dot3(a_mat[x], a_mat[x]) for x in ids]
    for _ in range(4):
        both = [_dot3(jnp.concatenate([inv[x], pw[x]], axis=0), pw[x]) for x in ids]
        inv = [inv[x] + both[x][:n] for x in ids]
        pw = [both[x][n:] for x in ids]
    inv = [inv[x] + _dot3(inv[x], pw[x]) for x in ids]
    ecum = [jnp.exp(cc[x]) for x in ids]
    uw = [_dot3(inv[x], jnp.concatenate([v[x] * beta[x], kb[x] * ecum[x]], axis=1)) for x in ids]
    uu = [uw[x][:, :GDN_D] for x in ids]
    ww = [uw[x][:, GDN_D:].astype(BF16) for x in ids]
    qg = [(q[x] * ecum[x]).astype(BF16) for x in ids]
    g_last0 = [cc[x][c - 1:c] for x in ids]
    g_last1 = [cc[x][n - 1:n] for x in ids]
    kdt = [(k[x] * jnp.exp(jnp.where(second, g_last1[x], g_last0[x]) - cc[x])).T.astype(BF16) for x in ids]

    s0 = [s_ref[bi, hd] for bi, hd in chains]
    s0b = [s0[x].astype(BF16) for x in ids]
    vn0 = [uu[x][:c] - _dot(ww[x][:c], s0b[x]) for x in ids]
    oi0 = [_dot(qg[x][:c], s0b[x]) for x in ids]
    s1 = [s0[x] * jnp.exp(g_last0[x]) + _dot(kdt[x], jnp.concatenate([vn0[x], zeros_c], axis=0).astype(BF16))
          for x in ids]
    s1b = [s1[x].astype(BF16) for x in ids]
    vn1 = [uu[x][c:] - _dot(ww[x][c:], s1b[x]) for x in ids]
    oi1 = [_dot(qg[x][c:], s1b[x]) for x in ids]
    s2 = [s1[x] * jnp.exp(g_last1[x]) + _dot(kdt[x], jnp.concatenate([zeros_c, vn1[x]], axis=0).astype(BF16))
          for x in ids]
    for x, (bi, hd) in enumerate(chains):
        s_ref[bi, hd] = s2[x]
        vn = jnp.concatenate([vn0[x], vn1[x]], axis=0).astype(BF16)
        o = jnp.concatenate([oi0[x], oi1[x]], axis=0) + _dot(qk[x], vn)
        o_ref[bi, :, lanes(hd, 0)] = _gated_norm(o, z_ref[bi, :, lanes(hd, 0)], nw_ref[...]).astype(BF16)

    @pl.when(step == pl.num_programs(1) - 1)
    def _():
        sout_ref[...] = s_ref[...]


def _gdn_prompt(u, gb, gbt, z, nw, ltri, ltrit):
    b, t, _ = u.shape
    n = GDN_TILE
    nb = math.gcd(b, GDN_BATCH_TILE)
    row = lambda i, j: (i, j, 0)
    const = lambda i, j: (0, 0)
    return pl.pallas_call(
        _gdn_chunk_kernel,
        name="gdn_prompt",
        grid=(b // nb, t // n),
        in_specs=[
            pl.BlockSpec((nb, n, CONV_DIM), row),
            pl.BlockSpec((nb, n, AB_COLS), row),
            pl.BlockSpec((nb, 8, n), lambda i, j: (i, 0, j)),
            pl.BlockSpec((nb, n, GDN_WIDTH), row),
            pl.BlockSpec((1, GDN_D), const),
            pl.BlockSpec((n, n), const),
            pl.BlockSpec((n, n), const),
        ],
        out_specs=[
            pl.BlockSpec((nb, n, GDN_WIDTH), row),
            pl.BlockSpec((nb, GDN_HEADS, GDN_D, GDN_D), lambda i, j: (i, 0, 0, 0)),
        ],
        out_shape=[
            jax.ShapeDtypeStruct((b, t, GDN_WIDTH), BF16),
            jax.ShapeDtypeStruct((b, GDN_HEADS, GDN_D, GDN_D), F32),
        ],
        scratch_shapes=[pltpu.VMEM((nb, GDN_HEADS, GDN_D, GDN_D), F32)],
        compiler_params=_cparams(("parallel", "arbitrary")),
    )(u, gb, gbt, z, nw, ltri, ltrit)


def _gdn_step_kernel(s_ref, u_ref, gb_ref, z_ref, nw_ref, *rest):
    sout_ref, o_ref = rest[-2:]
    batch = range(s_ref.shape[0])
    lanes = lambda hd, base: slice(base + hd * GDN_D, base + (hd + 1) * GDN_D)
    cols = [jnp.concatenate([u_ref[bi, :, lanes(hd, base)] for base in (GDN_WIDTH, 0) for hd in range(GDN_HEADS)],
                            axis=0).T for bi in batch]
    units = [(bi, hd) for bi in batch for hd in range(GDN_HEADS)]
    ids = range(len(units))
    kc = [cols[bi][:, hd:hd + 1] for bi, hd in units]
    qc = [cols[bi][:, GDN_HEADS + hd:GDN_HEADS + hd + 1] for bi, hd in units]
    s = [s_ref[bi, hd] * jnp.exp(gb_ref[bi, :, hd:hd + 1]) for bi, hd in units]
    ks = [jnp.sum(kc[x] * s[x], axis=0, keepdims=True) for x in ids]
    upd = [(u_ref[bi, :, lanes(hd, 2 * GDN_WIDTH)] - ks[x]) * gb_ref[bi, :, GDN_HEADS + hd:GDN_HEADS + hd + 1]
           for x, (bi, hd) in enumerate(units)]
    s = [s[x] + kc[x] * upd[x] for x in ids]
    o = [jnp.sum(qc[x] * s[x], axis=0, keepdims=True) for x in ids]
    for x, (bi, hd) in enumerate(units):
        sout_ref[0, bi, hd] = s[x]
        o_ref[bi, :, lanes(hd, 0)] = _gated_norm(o[x], z_ref[bi, :, lanes(hd, 0)], nw_ref[...]).astype(BF16)


def _gdn_sample(state, u, gb, z, nw, stacked, *, layer, depth):
    n = state.shape[0]
    bt = GDN_SAMPLE_TILE
    row = lambda i: (i, 0, 0)
    st = lambda i: (i, 0, 0, 0)
    in_specs = [
        pl.BlockSpec((bt, GDN_HEADS, GDN_D, GDN_D), st),
        pl.BlockSpec((bt, 1, CONV_DIM), row),
        pl.BlockSpec((bt, 1, AB_COLS), row),
        pl.BlockSpec((bt, 1, GDN_WIDTH), row),
        pl.BlockSpec((1, GDN_D), lambda i: (0, 0)),
    ]
    args = [state, u.reshape(n, 1, CONV_DIM), gb.reshape(n, 1, AB_COLS), z.reshape(n, 1, GDN_WIDTH), nw]
    aliases = {}
    if stacked is not None:
        aliases = {len(args): 0}
        in_specs.append(pl.BlockSpec(memory_space=pl.ANY))
        args.append(stacked)
    return pl.pallas_call(
        _gdn_step_kernel,
        name="gdn_sample",
        grid=(n // bt,),
        in_specs=in_specs,
        out_specs=[
            pl.BlockSpec((1, bt, GDN_HEADS, GDN_D, GDN_D), lambda i: (layer, i, 0, 0, 0)),
            pl.BlockSpec((bt, 1, GDN_WIDTH), row),
        ],
        out_shape=[
            jax.ShapeDtypeStruct((depth,) + state.shape, F32),
            jax.ShapeDtypeStruct((n, 1, GDN_WIDTH), BF16),
        ],
        input_output_aliases=aliases,
        compiler_params=_cparams(("parallel",)),
    )(*args)


def _neg_abs(x):
    bits = lax.bitcast_convert_type(x, jnp.uint32) | jnp.uint32(0x80000000)
    return lax.bitcast_convert_type(bits, F32)


def _sb_blocks(zns, later, carries, mask):
    stays = [[jnp.minimum(zn, 0.0) - jnp.log2(1.0 + jnp.exp2(_neg_abs(zn))) for zn in ch] for ch in zns]
    if mask is not None:
        stays = [[jnp.where(mask, s, 0.0) for s in ch] for ch in stays]
    scans = [[_dot(s.astype(BF16), later) for s in ch] for ch in stays]
    sums = [[jnp.sum(s, axis=1, keepdims=True) for s in ch] for ch in stays]
    ws, out_carries = [], []
    for c, carry in enumerate(carries):
        ws.append([])
        for zn, stay, scan, rs in zip(zns[c], stays[c], scans[c], sums[c]):
            w = jnp.exp2((stay - zn) + (scan + carry))
            if mask is not None:
                w = jnp.where(mask, w, 0.0)
            ws[c].append(w.astype(BF16))
            carry = carry + rs
        out_carries.append(carry)
    return ws, out_carries


def _sb_prompt_kernel(bias_ref, q_ref, kt_ref, v_ref, later_ref, o_ref, zn_ref, w_ref, acc_ref, car_ref, *,
                      last_rows):
    hp = pl.program_id(1)
    i = pl.program_id(2)
    nq = q_ref.shape[2]
    heads = range(SB_GROUP)

    def span(blk):
        return pl.ds(pl.multiple_of(blk * nq, nq), nq)

    def run(nr):
        rows = slice(0, nr)
        diag_mask = lax.broadcasted_iota(jnp.int32, (nr, nq), 1) < lax.broadcasted_iota(jnp.int32, (nr, nq), 0)

        def logits(blk, slot):
            for hh in heads:
                zn_ref[slot, hh, rows] = _dot(q_ref[0, hh, rows], kt_ref[0, hh, :, span(blk)]) + bias_ref[SB_GROUP * hp + hh]

        def weights(slot, mask):
            ws, carries = _sb_blocks([[zn_ref[slot, hh, rows]] for hh in heads], later_ref[...],
                                     [car_ref[hh, rows] for hh in heads], mask)
            for hh in heads:
                w_ref[hh, rows] = ws[hh][0]
                car_ref[hh, rows] = carries[hh]

        def attend(blk):
            for hh in heads:
                acc_ref[hh, rows] += _dot(w_ref[hh, rows], v_ref[0, hh, span(blk), :])

        def step(s, slot):
            attend(i - s + 1)
            logits(jnp.maximum(i - s - 1, 0), 1 - slot)
            weights(slot, None)

        acc_ref[...] = jnp.zeros_like(acc_ref)
        car_ref[...] = jnp.zeros_like(car_ref)
        logits(i, 0)
        logits(jnp.maximum(i - 1, 0), 1)
        weights(0, diag_mask)

        def body(p, _):
            step(2 * p + 1, 1)
            step(2 * p + 2, 0)
            return 0

        lax.fori_loop(0, i // 2, body, 0)

        @pl.when(i % 2 == 1)
        def _():
            step(i, 1)

        attend(0)
        o_ref[0] = jnp.concatenate([acc_ref[hh] for hh in heads], axis=1).astype(BF16)

    if last_rows == nq:
        run(nq)
    else:
        last = pl.num_programs(2) - 1
        pl.when(i < last)(lambda: run(nq))
        pl.when(i == last)(lambda: run(last_rows))


def _sb_prompt(q, kt, v, bias, later, *, tq, t_real):
    b, nh, t, dh = q.shape
    kt = kt.reshape(b, nh, dh, t)
    last_rows = min(tq, -(-(t_real - (t // tq - 1) * tq) // BF16_ROWS) * BF16_ROWS)
    return pl.pallas_call(
        functools.partial(_sb_prompt_kernel, last_rows=last_rows),
        name="sb_prompt",
        grid_spec=pltpu.PrefetchScalarGridSpec(
            num_scalar_prefetch=0,
            grid=(b, nh // SB_GROUP, t // tq),
            in_specs=[
                pl.BlockSpec(memory_space=pltpu.SMEM),
                pl.BlockSpec((1, SB_GROUP, tq, dh), lambda i, h, j: (i, h, j, 0)),
                pl.BlockSpec((1, SB_GROUP, dh, t), lambda i, h, j: (i, h, 0, 0)),
                pl.BlockSpec((1, SB_GROUP, t, dh), lambda i, h, j: (i, h, 0, 0)),
                pl.BlockSpec((tq, tq), lambda i, h, j: (0, 0)),
            ],
            out_specs=pl.BlockSpec((1, tq, SB_GROUP * dh), lambda i, h, j: (i, j, h)),
            scratch_shapes=[pltpu.VMEM((2, SB_GROUP, tq, tq), F32), pltpu.VMEM((SB_GROUP, tq, tq), BF16),
                            pltpu.VMEM((SB_GROUP, tq, dh), F32), pltpu.VMEM((SB_GROUP, tq, 1), F32)],
        ),
        out_shape=jax.ShapeDtypeStruct((b, t, nh * dh), BF16),
        compiler_params=_cparams(("parallel", "parallel", "arbitrary")),
    )(bias, q, kt, v, later)


def _sb_decode_kernel(pt_ref, q_ref, bias_ref, later_ref, *refs, n_pages):
    k_refs = refs[:n_pages]
    v_refs = refs[n_pages:2 * n_pages]
    o_ref = refs[2 * n_pages]
    head_of_lane = lax.broadcasted_iota(jnp.int32, (SB_HEADS, SB_WIDTH), 1) // SB_DH
    own = head_of_lane == lax.broadcasted_iota(jnp.int32, (SB_HEADS, SB_WIDTH), 0)
    qbd = jnp.where(own, q_ref[0], 0.0).astype(BF16)
    order = range(n_pages - 1, -1, -1)
    zs = [_dot(qbd, k_refs[p][0].astype(BF16)) + bias_ref[...] for p in order]
    ws, _ = _sb_blocks([zs], later_ref[...], [jnp.zeros((SB_HEADS, 1), F32)], None)
    acc = _dot_nt(ws[0][0], v_refs[order[0]][0].astype(BF16))
    for w, p in zip(ws[0][1:], order[1:]):
        acc = acc + _dot_nt(w, v_refs[p][0].astype(BF16))
    o_ref[0] = jnp.sum(jnp.where(own, acc, 0.0), axis=0, keepdims=True).astype(BF16)


def _sb_decode(q, cache_k, cache_v, page_table, bias_b, later, *, layer):
    n, n_pages = page_table.shape
    depth, n_pool, page, nh, dh = cache_k.shape
    ck = jnp.transpose(cache_k, (0, 1, 3, 4, 2)).reshape(depth * n_pool, nh * dh, page)
    cv = jnp.transpose(cache_v, (0, 1, 3, 4, 2)).reshape(depth * n_pool, nh * dh, page)

    def page_spec(p):
        return pl.BlockSpec((1, nh * dh, page), lambda i, pt: (layer * n_pool + pt[i * n_pages + p], 0, 0))

    row = lambda i, pt: (i, 0, 0)
    return pl.pallas_call(
        functools.partial(_sb_decode_kernel, n_pages=n_pages),
        name="sb_decode",
        grid_spec=pltpu.PrefetchScalarGridSpec(
            num_scalar_prefetch=1,
            grid=(n,),
            in_specs=[
                pl.BlockSpec((1, 1, nh * dh), row),
                pl.BlockSpec((nh, 128), lambda i, pt: (0, 0)),
                pl.BlockSpec((page, page), lambda i, pt: (0, 0)),
            ] + [page_spec(p) for p in range(n_pages)] * 2,
            out_specs=pl.BlockSpec((1, 1, nh * dh), row),
        ),
        out_shape=jax.ShapeDtypeStruct((n, 1, nh * dh), BF16),
        compiler_params=_cparams(("parallel",)),
    )(page_table.reshape(-1), q.reshape(n, 1, nh * dh), bias_b, later, *([ck] * n_pages), *([cv] * n_pages))


def _mlp_kernel(x_ref, oa_ref, ob_ref, wo_ref, nw_ref, wup_ref, wdn_ref, y_ref, x1_ref, xn_ref, acc_ref):
    j = pl.program_id(1)

    @pl.when(j == 0)
    def _():
        x1 = x_ref[...] + (_dot(oa_ref[...], wo_ref[:GDN_WIDTH]) + _dot(ob_ref[...], wo_ref[GDN_WIDTH:]))
        x1_ref[...] = x1
        ms = jnp.mean(x1 * x1, axis=-1, keepdims=True)
        xn_ref[...] = (x1 * lax.rsqrt(ms + EPS) * nw_ref[...]).astype(BF16)
        acc_ref[...] = jnp.zeros_like(acc_ref)

    h = jnp.maximum(_dot(xn_ref[...], wup_ref[...]), 0.0)
    acc_ref[...] += _dot((h * h).astype(BF16), wdn_ref[...])

    @pl.when(j == pl.num_programs(1) - 1)
    def _():
        y_ref[...] = x1_ref[...] + acc_ref[...]


def _mix_mlp(x, oa, ob, wo, nw, wup, wdn, *, tm, tf):
    m, d = x.shape
    ff = wup.shape[1]
    row = lambda i, j: (i, 0)
    const = lambda i, j: (0, 0)
    return pl.pallas_call(
        _mlp_kernel,
        name="mix_mlp",
        grid=(m // tm, ff // tf),
        in_specs=[
            pl.BlockSpec((tm, d), row),
            pl.BlockSpec((tm, GDN_WIDTH), row),
            pl.BlockSpec((tm, SB_WIDTH), row),
            pl.BlockSpec((d, d), const),
            pl.BlockSpec((1, d), const),
            pl.BlockSpec((d, tf), lambda i, j: (0, j)),
            pl.BlockSpec((tf, d), lambda i, j: (j, 0)),
        ],
        out_specs=pl.BlockSpec((tm, d), row),
        out_shape=jax.ShapeDtypeStruct((m, d), F32),
        scratch_shapes=[pltpu.VMEM((tm, d), F32), pltpu.VMEM((tm, d), BF16), pltpu.VMEM((tm, d), F32)],
        compiler_params=_cparams(("parallel", "arbitrary")),
    )(x, oa, ob, wo, nw, wup, wdn)


def _tri_consts(n_chunk_tile, chunk, tq, page):
    r = jnp.arange(n_chunk_tile)
    same = (r[:, None] // chunk) == (r[None, :] // chunk)
    ltri = (same & (r[None, :] <= r[:, None])).astype(BF16)

    def later(nk):
        k = jnp.arange(nk)
        return (k[:, None] > k[None, :]).astype(BF16)

    return ltri, ltri.T, later(tq), later(page)


def kernel(x_prompt, x_sample, cache_k, cache_v, page_table, state_gdn, state_conv, meta_tokens, norm1_w, w_in,
           conv_w, a_log, dt_bias, gdn_norm_w, q_norm_w, k_norm_w, sb_bias, w_o, norm2_w, w_up, w_down):
    depth, d_model = norm1_w.shape
    bp, seq, _ = x_prompt.shape
    bs = x_sample.shape[0]
    page = cache_k.shape[2]
    t_real = N_META + seq
    tp = -(-t_real // ROW_TILE) * ROW_TILE

    c0 = CONV_DIM
    c1 = c0 + GDN_WIDTH
    c2 = c1 + GDN_HEADS
    c3 = c2 + GDN_HEADS
    w_in_r = jnp.concatenate(
        [w_in[..., :c1], w_in[..., c3:], w_in[..., c1:c3],
         jnp.zeros((depth, d_model, AB_COLS - 2 * GDN_HEADS), w_in.dtype)], axis=-1).astype(BF16)
    w_o16, w_up16, w_dn16 = w_o.astype(BF16), w_up.astype(BF16), w_down.astype(BF16)
    pad_lanes = lambda a: jnp.pad(a, ((0, 0), (0, AB_COLS - a.shape[1])))[:, None, :]
    pa, pb = pad_lanes(a_log), pad_lanes(dt_bias)
    qnw = jnp.tile(q_norm_w, (1, SB_HEADS))[:, None, :]
    knw = jnp.tile(k_norm_w, (1, SB_HEADS))[:, None, :]
    lane_head = jnp.arange(SB_WIDTH) // SB_DH
    gseg = (lane_head[:, None] == lane_head[None, :]).astype(BF16)
    ltri, ltrit, later_q, later_p = _tri_consts(GDN_TILE, GDN_CHUNK, ROW_TILE, page)
    sb_bias2 = -sb_bias * LOG2E
    bias_b = jnp.broadcast_to(sb_bias2[:, :, None], (depth, SB_HEADS, 128))

    meta = jnp.broadcast_to(meta_tokens[None], (bp, N_META, d_model))
    xp = jnp.concatenate([meta, x_prompt, jnp.zeros((bp, tp - t_real, d_model), x_prompt.dtype)], axis=1)
    xs = x_sample.reshape(1, bs, d_model)
    st_conv = state_conv.reshape(depth, bs, (CONV_TAPS - 1) * CONV_DIM)

    kp_l, vp_l, sp_l, cp_l = [], [], [], []
    ks_l, vs_l, cs_l = [], [], []
    ss_all = None
    for l in range(depth):
        n1, n2 = norm1_w[l][None], norm2_w[l][None]
        gnw = gdn_norm_w[l][None]
        z, u, gb, gbt, ctail, q, kt, v, knew, vnew = _inproj_prompt(
            xp, n1, w_in_r[l], gseg, qnw[l], knw[l], conv_w[l], pa[l], pb[l], tm=ROW_TILE, t_real=t_real)
        o_a, s_fin = _gdn_prompt(u, gb, gbt, z, gnw, ltri, ltrit)
        o_b = _sb_prompt(q, kt, v, sb_bias2[l], later_q, tq=ROW_TILE, t_real=t_real)
        xp = _mix_mlp(xp.reshape(bp * tp, d_model), o_a.reshape(bp * tp, GDN_WIDTH), o_b.reshape(bp * tp, SB_WIDTH),
                      w_o16[l], n2, w_up16[l], w_dn16[l], tm=1024, tf=1024).reshape(bp, tp, d_model)
        kp_l.append(knew.reshape(bp, SB_HEADS, SB_DH, t_real).transpose(0, 3, 1, 2))
        vp_l.append(vnew.reshape(bp, SB_HEADS, SB_DH, t_real).transpose(0, 3, 1, 2))
        sp_l.append(s_fin)
        cp_l.append(ctail)

        conv_in, z, ab, q, knew, vnew = _inproj_sample(xs, n1, w_in_r[l], gseg, qnw[l], knw[l])
        u, gb, ns = _conv_sample(conv_in[0], st_conv[l], conv_w[l], ab[0], pa[l], pb[l])
        ss_all, o_a = _gdn_sample(state_gdn[l], u, gb, z[0], gnw, ss_all, layer=l, depth=depth)
        o_b = _sb_decode(q[0], cache_k, cache_v, page_table, bias_b[l], later_p, layer=l)
        xs = _mix_mlp(xs[0], o_a.reshape(bs, GDN_WIDTH), o_b.reshape(bs, SB_WIDTH), w_o16[l], n2, w_up16[l],
                      w_dn16[l], tm=bs, tf=1024)[None]
        ks_l.append(knew.reshape(bs, 1, SB_HEADS, SB_DH))
        vs_l.append(vnew.reshape(bs, 1, SB_HEADS, SB_DH))
        cs_l.append(ns.reshape(bs, CONV_TAPS - 1, CONV_DIM))

    y_prompt = xp[:, N_META:t_real]
    return (y_prompt, xs.reshape(bs, 1, d_model),
            jnp.stack(kp_l), jnp.stack(vp_l), jnp.stack(sp_l), jnp.stack(cp_l),
            jnp.stack(ks_l), jnp.stack(vs_l), ss_all, jnp.stack(cs_l))
```

```python
import functools
import math

import jax
import jax.numpy as jnp
from jax import lax
from jax.experimental import pallas as pl
from jax.experimental.pallas import tpu as pltpu

F32 = jnp.float32
BF16 = jnp.bfloat16
EPS = 1e-6
LOG2E = 1.4426950408889634

N_META = 16
GDN_HEADS = 4
GDN_D = 128
GDN_WIDTH = GDN_HEADS * GDN_D
CONV_DIM = 3 * GDN_WIDTH
CONV_TAPS = 4
GDN_CHUNK = 64
SB_HEADS = 8
SB_DH = 64
SB_WIDTH = SB_HEADS * SB_DH
AB_COLS = 128
C_CONV = 0
C_Z = C_CONV + CONV_DIM
C_SQ = C_Z + GDN_WIDTH
C_SK = C_SQ + SB_WIDTH
C_SV = C_SK + SB_WIDTH
C_AB = C_SV + SB_WIDTH
IN_COLS_PAD = C_AB + AB_COLS

BF16_ROWS = 16
ROW_TILE = 256
GDN_TILE = 2 * GDN_CHUNK
SB_GROUP = 8
GDN_BATCH_TILE = 4
GDN_SAMPLE_TILE = 8
VMEM_LIMIT = 56 * 1024 * 1024


def _cparams(sem):
    return pltpu.CompilerParams(dimension_semantics=sem, vmem_limit_bytes=VMEM_LIMIT)


def _dot(a, b):
    return jnp.dot(a, b, preferred_element_type=F32)


def _dot_nt(a, b):
    return lax.dot_general(a, b, (((1,), (1,)), ((), ())), preferred_element_type=F32)


def _split2(x):
    hi = x.astype(BF16)
    lo = (x - hi.astype(F32)).astype(BF16)
    return hi, lo


def _split3(x):
    hi = x.astype(BF16)
    r = x - hi.astype(F32)
    mid = r.astype(BF16)
    lo = (r - mid.astype(F32)).astype(BF16)
    return hi, mid, lo


def _dot3(a, b):
    ah, al = _split2(a)
    bh, bl = _split2(b)
    return _dot(ah, bh) + (_dot(ah, bl) + _dot(al, bh))


def _sigmoid(x):
    return 1.0 / (1.0 + jnp.exp(-x))


def _softplus(x):
    return jnp.maximum(x, 0.0) + jnp.log(1.0 + jnp.exp(-jnp.abs(x)))


def _inproj_core(x_ref, nw_ref, w_ref, gseg_ref, qnw_ref, knw_ref):
    x = x_ref[0]
    ms = jnp.mean(x * x, axis=-1, keepdims=True)
    h = (x * lax.rsqrt(ms + EPS) * nw_ref[...]).astype(BF16)

    def proj(lo, hi):
        return _dot(h, w_ref[:, lo:hi])

    def headnorm(s, wrow):
        hi, lo = _split2(s * s)
        ss = _dot(hi, gseg_ref[...]) + _dot(lo, gseg_ref[...])
        return s * lax.rsqrt(ss * (1.0 / SB_DH) + EPS) * wrow

    qn = headnorm(proj(C_SQ, C_SK), qnw_ref[...]) * (-(SB_DH ** -0.5) * LOG2E)
    kn = headnorm(proj(C_SK, C_SV), knw_ref[...])
    return proj, qn, kn, proj(C_SV, C_AB)


def _inproj_sample_kernel(x_ref, nw_ref, w_ref, gseg_ref, qnw_ref, knw_ref,
                          conv_ref, z_ref, ab_ref, q_ref, knew_ref, vnew_ref):
    proj, qn, kn, sv = _inproj_core(x_ref, nw_ref, w_ref, gseg_ref, qnw_ref, knw_ref)
    conv_ref[0] = proj(C_CONV, C_Z)
    z_ref[0] = proj(C_Z, C_SQ)
    ab_ref[0] = proj(C_AB, IN_COLS_PAD)
    q_ref[0] = qn
    knew_ref[0] = kn
    vnew_ref[0] = sv


def _inproj_prompt_kernel(x_ref, nw_ref, w_ref, gseg_ref, qnw_ref, knw_ref, cw_ref, pa_ref, pb_ref, *rest,
                          t_real, tail_tile, tail_row):
    (z_ref, u_ref, gb_ref, gbt_ref, ctail_ref, q_ref, kt_ref, v_ref, knew_ref, vnew_ref,
     xs_ref, hist_ref) = rest[-12:]
    j = pl.program_id(1)
    tm = x_ref.shape[1]
    proj, qn, kn, sv = _inproj_core(x_ref, nw_ref, w_ref, gseg_ref, qnw_ref, knw_ref)
    for hd in range(SB_HEADS):
        q_ref[0, hd] = qn[:, hd * SB_DH:(hd + 1) * SB_DH].astype(BF16)
        v_ref[0, hd] = sv[:, hd * SB_DH:(hd + 1) * SB_DH].astype(BF16)
    knt = kn.T
    kt_ref[0] = knt.astype(BF16)
    knew_ref[0, 0] = knt
    vnew_ref[0, 0] = sv.T
    z_ref[0] = proj(C_Z, C_SQ)

    conv = proj(C_CONV, C_Z)
    xs_ref[0:8] = jnp.where(j > 0, hist_ref[...], 0.0)
    xs_ref[8:] = conv
    hist_ref[...] = conv[tm - 8:]
    out = xs_ref[pl.ds(5, tm)] * cw_ref[0:1]
    for tap in range(1, CONV_TAPS):
        out = out + xs_ref[pl.ds(5 + tap, tm)] * cw_ref[tap:tap + 1]
    row = j * tm + lax.broadcasted_iota(jnp.int32, (tm, AB_COLS), 0)
    u, gb = _gdn_inputs(out, proj(C_AB, IN_COLS_PAD), pa_ref[...], pb_ref[...], row < t_real)
    u_ref[0] = u
    gb_ref[0] = gb
    gbt_ref[0] = gb.T[0:8]

    @pl.when(j == tail_tile)
    def _():
        ctail_ref[0] = conv[tail_row:tail_row + 8]


def _inproj_specs(tm, d):
    const = lambda i, j: (0, 0)
    return [
        pl.BlockSpec((1, tm, d), lambda i, j: (i, j, 0)),
        pl.BlockSpec((1, d), const),
        pl.BlockSpec((d, IN_COLS_PAD), const),
        pl.BlockSpec((SB_WIDTH, SB_WIDTH), const),
        pl.BlockSpec((1, SB_WIDTH), const),
        pl.BlockSpec((1, SB_WIDTH), const),
    ]


def _inproj_sample(x, nw, w, gseg, qnw, knw):
    b, t, d = x.shape
    widths = [CONV_DIM, GDN_WIDTH, AB_COLS, SB_WIDTH, SB_WIDTH, SB_WIDTH]
    return pl.pallas_call(
        _inproj_sample_kernel,
        name="inproj_sample",
        grid=(b, 1),
        in_specs=_inproj_specs(t, d),
        out_specs=[pl.BlockSpec((1, t, n), lambda i, j: (i, j, 0)) for n in widths],
        out_shape=[jax.ShapeDtypeStruct((b, t, n), F32) for n in widths],
        compiler_params=_cparams(("parallel", "parallel")),
    )(x, nw, w, gseg, qnw, knw)


def _inproj_prompt(x, nw, w, gseg, qnw, knw, cw, pa, pb, stacked, *, layer, depth, tm, t_real):
    b, t, d = x.shape
    row = lambda i, j: (i, j, 0)
    const = lambda i, j: (0, 0)
    hm = lambda i, j: (i, 0, j, 0)
    first_tail = t_real - (CONV_TAPS - 1)
    tail_tile = first_tail // tm
    tail_row = (first_tail % tm) // 8 * 8
    tail_off = first_tail % tm - tail_row
    assert tail_off + CONV_TAPS - 1 <= 8, "conv-state rows must sit in one 8-row group"
    f32_rows = lambda n: (jax.ShapeDtypeStruct((b, t, n), F32), pl.BlockSpec((1, tm, n), row))
    new_kv = (jax.ShapeDtypeStruct((depth, b, SB_WIDTH, t_real), F32),
              pl.BlockSpec((1, 1, SB_WIDTH, tm), lambda i, j: (layer, i, 0, j)))
    outs = [
        f32_rows(GDN_WIDTH), f32_rows(CONV_DIM), f32_rows(AB_COLS),
        (jax.ShapeDtypeStruct((b, 8, t), F32), pl.BlockSpec((1, 8, tm), lambda i, j: (i, 0, j))),
        (jax.ShapeDtypeStruct((b, 8, CONV_DIM), F32), pl.BlockSpec((1, 8, CONV_DIM), lambda i, j: (i, 0, 0))),
        (jax.ShapeDtypeStruct((b, SB_HEADS, t, SB_DH), BF16), pl.BlockSpec((1, SB_HEADS, tm, SB_DH), hm)),
        (jax.ShapeDtypeStruct((b, SB_WIDTH, t), BF16), pl.BlockSpec((1, SB_WIDTH, tm), lambda i, j: (i, 0, j))),
        (jax.ShapeDtypeStruct((b, SB_HEADS, t, SB_DH), BF16), pl.BlockSpec((1, SB_HEADS, tm, SB_DH), hm)),
        new_kv, new_kv,
    ]
    in_specs = _inproj_specs(tm, d) + [pl.BlockSpec((CONV_TAPS, CONV_DIM), const),
                                       pl.BlockSpec((1, AB_COLS), const), pl.BlockSpec((1, AB_COLS), const)]
    args = [x, nw, w, gseg, qnw, knw, cw, pa, pb]
    aliases = {}
    if stacked is not None:
        aliases = {len(args): len(outs) - 2, len(args) + 1: len(outs) - 1}
        in_specs += [pl.BlockSpec(memory_space=pl.ANY)] * 2
        args += list(stacked)
    res = pl.pallas_call(
        functools.partial(_inproj_prompt_kernel, t_real=t_real, tail_tile=tail_tile, tail_row=tail_row),
        name="inproj_prompt",
        grid=(b, t // tm),
        in_specs=in_specs,
        out_specs=[o[1] for o in outs],
        out_shape=[o[0] for o in outs],
        scratch_shapes=[pltpu.VMEM((tm + 8, CONV_DIM), F32), pltpu.VMEM((8, CONV_DIM), F32)],
        input_output_aliases=aliases,
        compiler_params=_cparams(("parallel", "arbitrary")),
    )(*args)
    z, u, gb, gbt, ctail, q, kt, v, knew, vnew = res
    return z, u, gb, gbt, ctail[:, tail_off:tail_off + CONV_TAPS - 1], q, kt, v, (knew, vnew)


def _gdn_inputs(conv_out, ab, pa, pb, valid):
    u = conv_out * _sigmoid(conv_out)
    parts = []
    for hd in range(2 * GDN_HEADS):
        sl = u[:, hd * GDN_D:(hd + 1) * GDN_D]
        nrm = sl * lax.rsqrt(jnp.sum(sl * sl, axis=-1, keepdims=True) + EPS)
        parts.append(nrm * (GDN_D ** -0.5) if hd < GDN_HEADS else nrm)
    parts.append(u[:, 2 * GDN_WIDTH:])
    g = -jnp.exp(pa) * _softplus(ab + pb)
    beta = _sigmoid(ab)
    lane = lax.broadcasted_iota(jnp.int32, ab.shape, 1)
    gb = jnp.where(lane < GDN_HEADS, g, jnp.where(lane < 2 * GDN_HEADS, beta, 0.0))
    if valid is not None:
        gb = jnp.where(valid, gb, 0.0)
    return jnp.concatenate(parts, axis=1), gb


def _conv_sample_kernel(x_ref, st_ref, cw_ref, ab_ref, pa_ref, pb_ref, u_ref, gb_ref, ns_ref):
    c = x_ref.shape[1]
    taps = [st_ref[:, k * c:(k + 1) * c] for k in range(CONV_TAPS - 1)] + [x_ref[...]]
    out = taps[0] * cw_ref[0:1]
    for tap in range(1, CONV_TAPS):
        out = out + taps[tap] * cw_ref[tap:tap + 1]
    u, gb = _gdn_inputs(out, ab_ref[...], pa_ref[...], pb_ref[...], None)
    u_ref[...] = u
    gb_ref[...] = gb
    for k in range(CONV_TAPS - 1):
        ns_ref[:, k * c:(k + 1) * c] = taps[k + 1]


def _conv_sample(conv_in, state, cw, ab, pa, pb):
    n, c = conv_in.shape
    return pl.pallas_call(
        _conv_sample_kernel,
        name="conv_sample",
        out_shape=[
            jax.ShapeDtypeStruct((n, c), F32),
            jax.ShapeDtypeStruct((n, AB_COLS), F32),
            jax.ShapeDtypeStruct((n, (CONV_TAPS - 1) * c), F32),
        ],
        compiler_params=pltpu.CompilerParams(vmem_limit_bytes=VMEM_LIMIT),
    )(conv_in, state, cw, ab, pa, pb)


def _gated_norm(o, z, nw):
    return o * lax.rsqrt(jnp.mean(o * o, axis=-1, keepdims=True) + EPS) * nw * (z * _sigmoid(z))


def _gdn_chunk_kernel(u_ref, gb_ref, gbt_ref, z_ref, nw_ref, ltri_ref, ltrit_ref, o_ref, sout_ref, s_ref):
    step = pl.program_id(1)
    n = GDN_TILE
    c = GDN_CHUNK

    @pl.when(step == 0)
    def _():
        s_ref[...] = jnp.zeros_like(s_ref)

    ltri = ltri_ref[...]
    ltrit = ltrit_ref[...]
    batch = range(u_ref.shape[0])
    gb = [gb_ref[bi] for bi in batch]
    gparts = [_split3(gb[bi]) for bi in batch]
    tparts = [_split3(gbt_ref[bi]) for bi in batch]
    cum_cols = [_dot(ltri, p[0]) + (_dot(ltri, p[1]) + _dot(ltri, p[2])) for p in gparts]
    cum_rows = [_dot(p[0], ltrit) + (_dot(p[1], ltrit) + _dot(p[2], ltrit)) for p in tparts]

    row = lax.broadcasted_iota(jnp.int32, (n, n), 0)
    col = lax.broadcasted_iota(jnp.int32, (n, n), 1)
    same = (row >= c) == (col >= c)
    causal = same & (row >= col)
    strict = same & (row > col)
    eye = (row == col).astype(F32)
    second = lax.broadcasted_iota(jnp.int32, (n, 1), 0) >= c
    zeros_c = jnp.zeros((c, GDN_D), F32)

    chains = [(bi, hd) for bi in batch for hd in range(GDN_HEADS)]
    ids = range(len(chains))
    lanes = lambda hd, base: slice(base + hd * GDN_D, base + (hd + 1) * GDN_D)
    q = [u_ref[bi, :, lanes(hd, 0)] for bi, hd in chains]
    k = [u_ref[bi, :, lanes(hd, GDN_WIDTH)] for bi, hd in chains]
    v = [u_ref[bi, :, lanes(hd, 2 * GDN_WIDTH)] for bi, hd in chains]
    cc = [cum_cols[bi][:, hd:hd + 1] for bi, hd in chains]
    cr = [cum_rows[bi][hd:hd + 1, :] for bi, hd in chains]
    beta = [gb[bi][:, GDN_HEADS + hd:GDN_HEADS + hd + 1] for bi, hd in chains]
    gam = [jnp.where(causal, jnp.exp(jnp.where(causal, cc[x] - cr[x], 0.0)), 0.0) for x in ids]
    kb = [k[x] * beta[x] for x in ids]
    k16 = [k[x].astype(BF16) for x in ids]
    a_mat = [jnp.where(strict, _dot_nt(kb[x].astype(BF16), k16[x]) * gam[x], 0.0) for x in ids]
    qk = [(_dot_nt(q[x].astype(BF16), k16[x]) * gam[x]).astype(BF16) for x in ids]
    inv = [eye - a_mat[x] for x in ids]
    pw = [_dot3(a_mat[x], a_mat[x]) for x in ids]
    for _ in range(4):
        both = [_dot3(jnp.concatenate([inv[x], pw[x]], axis=0), pw[x]) for x in ids]
        inv = [inv[x] + both[x][:n] for x in ids]
        pw = [both[x][n:] for x in ids]
    inv = [inv[x] + _dot3(inv[x], pw[x]) for x in ids]
    ecum = [jnp.exp(cc[x]) for x in ids]
    uw = [_dot3(inv[x], jnp.concatenate([v[x] * beta[x], kb[x] * ecum[x]], axis=1)) for x in ids]
    uu = [uw[x][:, :GDN_D] for x in ids]
    ww = [uw[x][:, GDN_D:].astype(BF16) for x in ids]
    qg = [(q[x] * ecum[x]).astype(BF16) for x in ids]
    g_last0 = [cc[x][c - 1:c] for x in ids]
    g_last1 = [cc[x][n - 1:n] for x in ids]
    kdt = [(k[x] * jnp.exp(jnp.where(second, g_last1[x], g_last0[x]) - cc[x])).T.astype(BF16) for x in ids]

    s0 = [s_ref[bi, hd] for bi, hd in chains]
    s0b = [s0[x].astype(BF16) for x in ids]
    vn0 = [uu[x][:c] - _dot(ww[x][:c], s0b[x]) for x in ids]
    oi0 = [_dot(qg[x][:c], s0b[x]) for x in ids]
    s1 = [s0[x] * jnp.exp(g_last0[x]) + _dot(kdt[x], jnp.concatenate([vn0[x], zeros_c], axis=0).astype(BF16))
          for x in ids]
    s1b = [s1[x].astype(BF16) for x in ids]
    vn1 = [uu[x][c:] - _dot(ww[x][c:], s1b[x]) for x in ids]
    oi1 = [_dot(qg[x][c:], s1b[x]) for x in ids]
    s2 = [s1[x] * jnp.exp(g_last1[x]) + _dot(kdt[x], jnp.concatenate([zeros_c, vn1[x]], axis=0).astype(BF16))
          for x in ids]
    for x, (bi, hd) in enumerate(chains):
        s_ref[bi, hd] = s2[x]
        vn = jnp.concatenate([vn0[x], vn1[x]], axis=0).astype(BF16)
        o = jnp.concatenate([oi0[x], oi1[x]], axis=0) + _dot(qk[x], vn)
        o_ref[bi, :, lanes(hd, 0)] = _gated_norm(o, z_ref[bi, :, lanes(hd, 0)], nw_ref[...]).astype(BF16)

    @pl.when(step == pl.num_programs(1) - 1)
    def _():
        sout_ref[...] = s_ref[...]


def _gdn_prompt(u, gb, gbt, z, nw, ltri, ltrit):
    b, t, _ = u.shape
    n = GDN_TILE
    nb = math.gcd(b, GDN_BATCH_TILE)
    row = lambda i, j: (i, j, 0)
    const = lambda i, j: (0, 0)
    return pl.pallas_call(
        _gdn_chunk_kernel,
        name="gdn_prompt",
        grid=(b // nb, t // n),
        in_specs=[
            pl.BlockSpec((nb, n, CONV_DIM), row),
            pl.BlockSpec((nb, n, AB_COLS), row),
            pl.BlockSpec((nb, 8, n), lambda i, j: (i, 0, j)),
            pl.BlockSpec((nb, n, GDN_WIDTH), row),
            pl.BlockSpec((1, GDN_D), const),
            pl.BlockSpec((n, n), const),
            pl.BlockSpec((n, n), const),
        ],
        out_specs=[
            pl.BlockSpec((nb, n, GDN_WIDTH), row),
            pl.BlockSpec((nb, GDN_HEADS, GDN_D, GDN_D), lambda i, j: (i, 0, 0, 0)),
        ],
        out_shape=[
            jax.ShapeDtypeStruct((b, t, GDN_WIDTH), BF16),
            jax.ShapeDtypeStruct((b, GDN_HEADS, GDN_D, GDN_D), F32),
        ],
        scratch_shapes=[pltpu.VMEM((nb, GDN_HEADS, GDN_D, GDN_D), F32)],
        compiler_params=_cparams(("parallel", "arbitrary")),
    )(u, gb, gbt, z, nw, ltri, ltrit)


def _gdn_step_kernel(s_ref, u_ref, gb_ref, z_ref, nw_ref, *rest):
    sout_ref, o_ref = rest[-2:]
    batch = range(s_ref.shape[0])
    lanes = lambda hd, base: slice(base + hd * GDN_D, base + (hd + 1) * GDN_D)
    cols = [jnp.concatenate([u_ref[bi, :, lanes(hd, base)] for base in (GDN_WIDTH, 0) for hd in range(GDN_HEADS)],
                            axis=0).T for bi in batch]
    units = [(bi, hd) for bi in batch for hd in range(GDN_HEADS)]
    ids = range(len(units))
    kc = [cols[bi][:, hd:hd + 1] for bi, hd in units]
    qc = [cols[bi][:, GDN_HEADS + hd:GDN_HEADS + hd + 1] for bi, hd in units]
    s = [s_ref[bi, hd] * jnp.exp(gb_ref[bi, :, hd:hd + 1]) for bi, hd in units]
    ks = [jnp.sum(kc[x] * s[x], axis=0, keepdims=True) for x in ids]
    upd = [(u_ref[bi, :, lanes(hd, 2 * GDN_WIDTH)] - ks[x]) * gb_ref[bi, :, GDN_HEADS + hd:GDN_HEADS + hd + 1]
           for x, (bi, hd) in enumerate(units)]
    s = [s[x] + kc[x] * upd[x] for x in ids]
    o = [jnp.sum(qc[x] * s[x], axis=0, keepdims=True) for x in ids]
    for x, (bi, hd) in enumerate(units):
        sout_ref[0, bi, hd] = s[x]
        o_ref[bi, :, lanes(hd, 0)] = _gated_norm(o[x], z_ref[bi, :, lanes(hd, 0)], nw_ref[...]).astype(BF16)


def _gdn_sample(state, u, gb, z, nw, stacked, *, layer, depth):
    n = state.shape[0]
    bt = GDN_SAMPLE_TILE
    row = lambda i: (i, 0, 0)
    st = lambda i: (i, 0, 0, 0)
    in_specs = [
        pl.BlockSpec((bt, GDN_HEADS, GDN_D, GDN_D), st),
        pl.BlockSpec((bt, 1, CONV_DIM), row),
        pl.BlockSpec((bt, 1, AB_COLS), row),
        pl.BlockSpec((bt, 1, GDN_WIDTH), row),
        pl.BlockSpec((1, GDN_D), lambda i: (0, 0)),
    ]
    args = [state, u.reshape(n, 1, CONV_DIM), gb.reshape(n, 1, AB_COLS), z.reshape(n, 1, GDN_WIDTH), nw]
    aliases = {}
    if stacked is not None:
        aliases = {len(args): 0}
        in_specs.append(pl.BlockSpec(memory_space=pl.ANY))
        args.append(stacked)
    return pl.pallas_call(
        _gdn_step_kernel,
        name="gdn_sample",
        grid=(n // bt,),
        in_specs=in_specs,
        out_specs=[
            pl.BlockSpec((1, bt, GDN_HEADS, GDN_D, GDN_D), lambda i: (layer, i, 0, 0, 0)),
            pl.BlockSpec((bt, 1, GDN_WIDTH), row),
        ],
        out_shape=[
            jax.ShapeDtypeStruct((depth,) + state.shape, F32),
            jax.ShapeDtypeStruct((n, 1, GDN_WIDTH), BF16),
        ],
        input_output_aliases=aliases,
        compiler_params=_cparams(("parallel",)),
    )(*args)


def _neg_abs(x):
    bits = lax.bitcast_convert_type(x, jnp.uint32) | jnp.uint32(0x80000000)
    return lax.bitcast_convert_type(bits, F32)


def _sb_blocks(zns, later, carries, mask):
    stays = [[jnp.minimum(zn, 0.0) - jnp.log2(1.0 + jnp.exp2(_neg_abs(zn))) for zn in ch] for ch in zns]
    if mask is not None:
        stays = [[jnp.where(mask, s, 0.0) for s in ch] for ch in stays]
    scans = [[_dot(s.astype(BF16), later) for s in ch] for ch in stays]
    sums = [[jnp.sum(s, axis=1, keepdims=True) for s in ch] for ch in stays]
    ws, out_carries = [], []
    for c, carry in enumerate(carries):
        ws.append([])
        for zn, stay, scan, rs in zip(zns[c], stays[c], scans[c], sums[c]):
            w = jnp.exp2((stay - zn) + (scan + carry))
            if mask is not None:
                w = jnp.where(mask, w, 0.0)
            ws[c].append(w.astype(BF16))
            carry = carry + rs
        out_carries.append(carry)
    return ws, out_carries


def _sb_prompt_kernel(bias_ref, q_ref, kt_ref, v_ref, later_ref, o_ref, zn_ref, w_ref, acc_ref, car_ref, *,
                      last_rows):
    hp = pl.program_id(1)
    i = pl.program_id(2)
    nq = q_ref.shape[2]
    heads = range(SB_GROUP)

    def span(blk):
        return pl.ds(pl.multiple_of(blk * nq, nq), nq)

    def run(nr):
        rows = slice(0, nr)
        diag_mask = lax.broadcasted_iota(jnp.int32, (nr, nq), 1) < lax.broadcasted_iota(jnp.int32, (nr, nq), 0)

        def logits(blk, slot):
            for hh in heads:
                zn_ref[slot, hh, rows] = _dot(q_ref[0, hh, rows], kt_ref[0, hh, :, span(blk)]) + bias_ref[SB_GROUP * hp + hh]

        def weights(slot, mask):
            ws, carries = _sb_blocks([[zn_ref[slot, hh, rows]] for hh in heads], later_ref[...],
                                     [car_ref[hh, rows] for hh in heads], mask)
            for hh in heads:
                w_ref[hh, rows] = ws[hh][0]
                car_ref[hh, rows] = carries[hh]

        def attend(blk):
            for hh in heads:
                acc_ref[hh, rows] += _dot(w_ref[hh, rows], v_ref[0, hh, span(blk), :])

        def step(s, slot):
            attend(i - s + 1)
            logits(jnp.maximum(i - s - 1, 0), 1 - slot)
            weights(slot, None)

        acc_ref[...] = jnp.zeros_like(acc_ref)
        car_ref[...] = jnp.zeros_like(car_ref)
        logits(i, 0)
        logits(jnp.maximum(i - 1, 0), 1)
        weights(0, diag_mask)

        def body(p, _):
            step(2 * p + 1, 1)
            step(2 * p + 2, 0)
            return 0

        lax.fori_loop(0, i // 2, body, 0)

        @pl.when(i % 2 == 1)
        def _():
            step(i, 1)

        attend(0)
        o_ref[0] = jnp.concatenate([acc_ref[hh] for hh in heads], axis=1).astype(BF16)

    if last_rows == nq:
        run(nq)
    else:
        last = pl.num_programs(2) - 1
        pl.when(i < last)(lambda: run(nq))
        pl.when(i == last)(lambda: run(last_rows))


def _sb_prompt(q, kt, v, bias, later, *, tq, t_real):
    b, nh, t, dh = q.shape
    kt = kt.reshape(b, nh, dh, t)
    last_rows = min(tq, -(-(t_real - (t // tq - 1) * tq) // BF16_ROWS) * BF16_ROWS)
    return pl.pallas_call(
        functools.partial(_sb_prompt_kernel, last_rows=last_rows),
        name="sb_prompt",
        grid_spec=pltpu.PrefetchScalarGridSpec(
            num_scalar_prefetch=0,
            grid=(b, nh // SB_GROUP, t // tq),
            in_specs=[
                pl.BlockSpec(memory_space=pltpu.SMEM),
                pl.BlockSpec((1, SB_GROUP, tq, dh), lambda i, h, j: (i, h, j, 0)),
                pl.BlockSpec((1, SB_GROUP, dh, t), lambda i, h, j: (i, h, 0, 0)),
                pl.BlockSpec((1, SB_GROUP, t, dh), lambda i, h, j: (i, h, 0, 0)),
                pl.BlockSpec((tq, tq), lambda i, h, j: (0, 0)),
            ],
            out_specs=pl.BlockSpec((1, tq, SB_GROUP * dh), lambda i, h, j: (i, j, h)),
            scratch_shapes=[pltpu.VMEM((2, SB_GROUP, tq, tq), F32), pltpu.VMEM((SB_GROUP, tq, tq), BF16),
                            pltpu.VMEM((SB_GROUP, tq, dh), F32), pltpu.VMEM((SB_GROUP, tq, 1), F32)],
        ),
        out_shape=jax.ShapeDtypeStruct((b, t, nh * dh), BF16),
        compiler_params=_cparams(("parallel", "parallel", "arbitrary")),
    )(bias, q, kt, v, later)


def _sb_decode_kernel(pt_ref, q_ref, bias_ref, later_ref, *refs, n_pages):
    k_refs = refs[:n_pages]
    v_refs = refs[n_pages:2 * n_pages]
    o_ref = refs[2 * n_pages]
    head_of_lane = lax.broadcasted_iota(jnp.int32, (SB_HEADS, SB_WIDTH), 1) // SB_DH
    own = head_of_lane == lax.broadcasted_iota(jnp.int32, (SB_HEADS, SB_WIDTH), 0)
    qbd = jnp.where(own, q_ref[0], 0.0).astype(BF16)
    order = range(n_pages - 1, -1, -1)
    zs = [_dot(qbd, k_refs[p][0].astype(BF16)) + bias_ref[...] for p in order]
    ws, _ = _sb_blocks([zs], later_ref[...], [jnp.zeros((SB_HEADS, 1), F32)], None)
    acc = _dot_nt(ws[0][0], v_refs[order[0]][0].astype(BF16))
    for w, p in zip(ws[0][1:], order[1:]):
        acc = acc + _dot_nt(w, v_refs[p][0].astype(BF16))
    o_ref[0] = jnp.sum(jnp.where(own, acc, 0.0), axis=0, keepdims=True).astype(BF16)


def _sb_decode(q, cache_k, cache_v, page_table, bias_b, later, *, layer):
    n, n_pages = page_table.shape
    depth, n_pool, page, nh, dh = cache_k.shape
    ck = jnp.transpose(cache_k, (0, 1, 3, 4, 2)).reshape(depth * n_pool, nh * dh, page)
    cv = jnp.transpose(cache_v, (0, 1, 3, 4, 2)).reshape(depth * n_pool, nh * dh, page)

    def page_spec(p):
        return pl.BlockSpec((1, nh * dh, page), lambda i, pt: (layer * n_pool + pt[i * n_pages + p], 0, 0))

    row = lambda i, pt: (i, 0, 0)
    return pl.pallas_call(
        functools.partial(_sb_decode_kernel, n_pages=n_pages),
        name="sb_decode",
        grid_spec=pltpu.PrefetchScalarGridSpec(
            num_scalar_prefetch=1,
            grid=(n,),
            in_specs=[
                pl.BlockSpec((1, 1, nh * dh), row),
                pl.BlockSpec((nh, 128), lambda i, pt: (0, 0)),
                pl.BlockSpec((page, page), lambda i, pt: (0, 0)),
            ] + [page_spec(p) for p in range(n_pages)] * 2,
            out_specs=pl.BlockSpec((1, 1, nh * dh), row),
        ),
        out_shape=jax.ShapeDtypeStruct((n, 1, nh * dh), BF16),
        compiler_params=_cparams(("parallel",)),
    )(page_table.reshape(-1), q.reshape(n, 1, nh * dh), bias_b, later, *([ck] * n_pages), *([cv] * n_pages))


def _mlp_kernel(x_ref, oa_ref, ob_ref, wo_ref, nw_ref, wup_ref, wdn_ref, y_ref, x1_ref, xn_ref, acc_ref):
    j = pl.program_id(1)

    @pl.when(j == 0)
    def _():
        x1 = x_ref[...] + (_dot(oa_ref[...], wo_ref[:GDN_WIDTH]) + _dot(ob_ref[...], wo_ref[GDN_WIDTH:]))
        x1_ref[...] = x1
        ms = jnp.mean(x1 * x1, axis=-1, keepdims=True)
        xn_ref[...] = (x1 * lax.rsqrt(ms + EPS) * nw_ref[...]).astype(BF16)
        acc_ref[...] = jnp.zeros_like(acc_ref)

    h = jnp.maximum(_dot(xn_ref[...], wup_ref[...]), 0.0)
    acc_ref[...] += _dot((h * h).astype(BF16), wdn_ref[...])

    @pl.when(j == pl.num_programs(1) - 1)
    def _():
        y_ref[...] = x1_ref[...] + acc_ref[...]


def _mix_mlp(x, oa, ob, wo, nw, wup, wdn, *, tm, tf):
    m, d = x.shape
    ff = wup.shape[1]
    row = lambda i, j: (i, 0)
    const = lambda i, j: (0, 0)
    return pl.pallas_call(
        _mlp_kernel,
        name="mix_mlp",
        grid=(m // tm, ff // tf),
        in_specs=[
            pl.BlockSpec((tm, d), row),
            pl.BlockSpec((tm, GDN_WIDTH), row),
            pl.BlockSpec((tm, SB_WIDTH), row),
            pl.BlockSpec((d, d), const),
            pl.BlockSpec((1, d), const),
            pl.BlockSpec((d, tf), lambda i, j: (0, j)),
            pl.BlockSpec((tf, d), lambda i, j: (j, 0)),
        ],
        out_specs=pl.BlockSpec((tm, d), row),
        out_shape=jax.ShapeDtypeStruct((m, d), F32),
        scratch_shapes=[pltpu.VMEM((tm, d), F32), pltpu.VMEM((tm, d), BF16), pltpu.VMEM((tm, d), F32)],
        compiler_params=_cparams(("parallel", "arbitrary")),
    )(x, oa, ob, wo, nw, wup, wdn)


def _tri_consts(n_chunk_tile, chunk, tq, page):
    r = jnp.arange(n_chunk_tile)
    same = (r[:, None] // chunk) == (r[None, :] // chunk)
    ltri = (same & (r[None, :] <= r[:, None])).astype(BF16)

    def later(nk):
        k = jnp.arange(nk)
        return (k[:, None] > k[None, :]).astype(BF16)

    return ltri, ltri.T, later(tq), later(page)


def kernel(x_prompt, x_sample, cache_k, cache_v, page_table, state_gdn, state_conv, meta_tokens, norm1_w, w_in,
           conv_w, a_log, dt_bias, gdn_norm_w, q_norm_w, k_norm_w, sb_bias, w_o, norm2_w, w_up, w_down):
    depth, d_model = norm1_w.shape
    bp, seq, _ = x_prompt.shape
    bs = x_sample.shape[0]
    page = cache_k.shape[2]
    t_real = N_META + seq
    tp = -(-t_real // ROW_TILE) * ROW_TILE

    c0 = CONV_DIM
    c1 = c0 + GDN_WIDTH
    c2 = c1 + GDN_HEADS
    c3 = c2 + GDN_HEADS
    w_in_r = jnp.concatenate(
        [w_in[..., :c1], w_in[..., c3:], w_in[..., c1:c3],
         jnp.zeros((depth, d_model, AB_COLS - 2 * GDN_HEADS), w_in.dtype)], axis=-1).astype(BF16)
    w_o16, w_up16, w_dn16 = w_o.astype(BF16), w_up.astype(BF16), w_down.astype(BF16)
    pad_lanes = lambda a: jnp.pad(a, ((0, 0), (0, AB_COLS - a.shape[1])))[:, None, :]
    pa, pb = pad_lanes(a_log), pad_lanes(dt_bias)
    qnw = jnp.tile(q_norm_w, (1, SB_HEADS))[:, None, :]
    knw = jnp.tile(k_norm_w, (1, SB_HEADS))[:, None, :]
    lane_head = jnp.arange(SB_WIDTH) // SB_DH
    gseg = (lane_head[:, None] == lane_head[None, :]).astype(BF16)
    ltri, ltrit, later_q, later_p = _tri_consts(GDN_TILE, GDN_CHUNK, ROW_TILE, page)
    sb_bias2 = -sb_bias * LOG2E
    bias_b = jnp.broadcast_to(sb_bias2[:, :, None], (depth, SB_HEADS, 128))

    meta = jnp.broadcast_to(meta_tokens[None], (bp, N_META, d_model))
    xp = jnp.concatenate([meta, x_prompt, jnp.zeros((bp, tp - t_real, d_model), x_prompt.dtype)], axis=1)
    xs = x_sample.reshape(1, bs, d_model)
    st_conv = state_conv.reshape(depth, bs, (CONV_TAPS - 1) * CONV_DIM)

    sp_l, cp_l = [], []
    kv_prompt = None
    ks_l, vs_l, cs_l = [], [], []
    ss_all = None
    for l in range(depth):
        n1, n2 = norm1_w[l][None], norm2_w[l][None]
        gnw = gdn_norm_w[l][None]
        z, u, gb, gbt, ctail, q, kt, v, kv_prompt = _inproj_prompt(
            xp, n1, w_in_r[l], gseg, qnw[l], knw[l], conv_w[l], pa[l], pb[l], kv_prompt,
            layer=l, depth=depth, tm=ROW_TILE, t_real=t_real)
        o_a, s_fin = _gdn_prompt(u, gb, gbt, z, gnw, ltri, ltrit)
        o_b = _sb_prompt(q, kt, v, sb_bias2[l], later_q, tq=ROW_TILE, t_real=t_real)
        xp = _mix_mlp(xp.reshape(bp * tp, d_model), o_a.reshape(bp * tp, GDN_WIDTH), o_b.reshape(bp * tp, SB_WIDTH),
                      w_o16[l], n2, w_up16[l], w_dn16[l], tm=1024, tf=1024).reshape(bp, tp, d_model)
        sp_l.append(s_fin)
        cp_l.append(ctail)

        conv_in, z, ab, q, knew, vnew = _inproj_sample(xs, n1, w_in_r[l], gseg, qnw[l], knw[l])
        u, gb, ns = _conv_sample(conv_in[0], st_conv[l], conv_w[l], ab[0], pa[l], pb[l])
        ss_all, o_a = _gdn_sample(state_gdn[l], u, gb, z[0], gnw, ss_all, layer=l, depth=depth)
        o_b = _sb_decode(q[0], cache_k, cache_v, page_table, bias_b[l], later_p, layer=l)
        xs = _mix_mlp(xs[0], o_a.reshape(bs, GDN_WIDTH), o_b.reshape(bs, SB_WIDTH), w_o16[l], n2, w_up16[l],
                      w_dn16[l], tm=bs, tf=1024)[None]
        ks_l.append(knew.reshape(bs, 1, SB_HEADS, SB_DH))
        vs_l.append(vnew.reshape(bs, 1, SB_HEADS, SB_DH))
        cs_l.append(ns.reshape(bs, CONV_TAPS - 1, CONV_DIM))

    heads_last = lambda a: a.reshape(depth, bp, SB_HEADS, SB_DH, t_real).transpose(0, 1, 4, 2, 3)
    y_prompt = xp[:, N_META:t_real]
    return (y_prompt, xs.reshape(bs, 1, d_model),
            heads_last(kv_prompt[0]), heads_last(kv_prompt[1]), jnp.stack(sp_l), jnp.stack(cp_l),
            jnp.stack(ks_l), jnp.stack(vs_l), ss_all, jnp.stack(cs_l))
```

```python
import functools
import math

import jax
import jax.numpy as jnp
from jax import lax
from jax.experimental import pallas as pl
from jax.experimental.pallas import tpu as pltpu

F32 = jnp.float32
BF16 = jnp.bfloat16
EPS = 1e-6
LOG2E = 1.4426950408889634

N_META = 16
GDN_HEADS = 4
GDN_D = 128
GDN_WIDTH = GDN_HEADS * GDN_D
CONV_DIM = 3 * GDN_WIDTH
CONV_TAPS = 4
GDN_CHUNK = 64
SB_HEADS = 8
SB_DH = 64
SB_WIDTH = SB_HEADS * SB_DH
AB_COLS = 128
C_CONV = 0
C_Z = C_CONV + CONV_DIM
C_SQ = C_Z + GDN_WIDTH
C_SK = C_SQ + SB_WIDTH
C_SV = C_SK + SB_WIDTH
C_AB = C_SV + SB_WIDTH
IN_COLS_PAD = C_AB + AB_COLS

BF16_ROWS = 16
ROW_TILE = 256
GDN_TILE = 2 * GDN_CHUNK
SB_GROUP = 8
GDN_BATCH_TILE = 4
GDN_SAMPLE_TILE = 8
VMEM_LIMIT = 56 * 1024 * 1024


def _cparams(sem):
    return pltpu.CompilerParams(dimension_semantics=sem, vmem_limit_bytes=VMEM_LIMIT)


def _dot(a, b):
    return jnp.dot(a, b, preferred_element_type=F32)


def _dot_nt(a, b):
    return lax.dot_general(a, b, (((1,), (1,)), ((), ())), preferred_element_type=F32)


def _split2(x):
    hi = x.astype(BF16)
    lo = (x - hi.astype(F32)).astype(BF16)
    return hi, lo


def _split3(x):
    hi = x.astype(BF16)
    r = x - hi.astype(F32)
    mid = r.astype(BF16)
    lo = (r - mid.astype(F32)).astype(BF16)
    return hi, mid, lo


def _dot3(a, b):
    ah, al = _split2(a)
    bh, bl = _split2(b)
    return _dot(ah, bh) + (_dot(ah, bl) + _dot(al, bh))


def _sigmoid(x):
    return 1.0 / (1.0 + jnp.exp(-x))


def _softplus(x):
    return jnp.maximum(x, 0.0) + jnp.log(1.0 + jnp.exp(-jnp.abs(x)))


def _inproj_core(x_ref, nw_ref, w_ref, gseg_ref, qnw_ref, knw_ref):
    x = x_ref[0]
    ms = jnp.mean(x * x, axis=-1, keepdims=True)
    h = (x * lax.rsqrt(ms + EPS) * nw_ref[...]).astype(BF16)

    def proj(lo, hi):
        return _dot(h, w_ref[:, lo:hi])

    def headnorm(s, wrow):
        hi, lo = _split2(s * s)
        ss = _dot(hi, gseg_ref[...]) + _dot(lo, gseg_ref[...])
        return s * lax.rsqrt(ss * (1.0 / SB_DH) + EPS) * wrow

    qn = headnorm(proj(C_SQ, C_SK), qnw_ref[...]) * (-(SB_DH ** -0.5) * LOG2E)
    kn = headnorm(proj(C_SK, C_SV), knw_ref[...])
    return proj, qn, kn, proj(C_SV, C_AB)


def _inproj_sample_kernel(x_ref, nw_ref, w_ref, gseg_ref, qnw_ref, knw_ref,
                          conv_ref, z_ref, ab_ref, q_ref, knew_ref, vnew_ref):
    proj, qn, kn, sv = _inproj_core(x_ref, nw_ref, w_ref, gseg_ref, qnw_ref, knw_ref)
    conv_ref[0] = proj(C_CONV, C_Z)
    z_ref[0] = proj(C_Z, C_SQ)
    ab_ref[0] = proj(C_AB, IN_COLS_PAD)
    q_ref[0] = qn
    knew_ref[0] = kn
    vnew_ref[0] = sv


def _inproj_prompt_kernel(x_ref, nw_ref, w_ref, gseg_ref, qnw_ref, knw_ref, cw_ref, pa_ref, pb_ref, *rest,
                          t_real, tail_tile, tail_row):
    (z_ref, u_ref, gb_ref, gbt_ref, ctail_ref, q_ref, kt_ref, v_ref, knew_ref, vnew_ref,
     xs_ref, hist_ref) = rest[-12:]
    j = pl.program_id(1)
    tm = x_ref.shape[1]
    proj, qn, kn, sv = _inproj_core(x_ref, nw_ref, w_ref, gseg_ref, qnw_ref, knw_ref)
    for hd in range(SB_HEADS):
        q_ref[0, hd] = qn[:, hd * SB_DH:(hd + 1) * SB_DH].astype(BF16)
        v_ref[0, hd] = sv[:, hd * SB_DH:(hd + 1) * SB_DH].astype(BF16)
    knt = kn.T
    kt_ref[0] = knt.astype(BF16)
    knew_ref[0, 0] = knt
    vnew_ref[0, 0] = sv.T
    z_ref[0] = proj(C_Z, C_SQ)

    conv = proj(C_CONV, C_Z)
    xs_ref[0:8] = jnp.where(j > 0, hist_ref[...], 0.0)
    xs_ref[8:] = conv
    hist_ref[...] = conv[tm - 8:]
    out = xs_ref[pl.ds(5, tm)] * cw_ref[0:1]
    for tap in range(1, CONV_TAPS):
        out = out + xs_ref[pl.ds(5 + tap, tm)] * cw_ref[tap:tap + 1]
    row = j * tm + lax.broadcasted_iota(jnp.int32, (tm, AB_COLS), 0)
    u, gb = _gdn_inputs(out, proj(C_AB, IN_COLS_PAD), pa_ref[...], pb_ref[...], row < t_real)
    u_ref[0] = u
    gb_ref[0] = gb
    gbt_ref[0] = gb.T[0:8]

    @pl.when(j == tail_tile)
    def _():
        ctail_ref[0] = conv[tail_row:tail_row + 8]


def _inproj_specs(tm, d):
    const = lambda i, j: (0, 0)
    return [
        pl.BlockSpec((1, tm, d), lambda i, j: (i, j, 0)),
        pl.BlockSpec((1, d), const),
        pl.BlockSpec((d, IN_COLS_PAD), const),
        pl.BlockSpec((SB_WIDTH, SB_WIDTH), const),
        pl.BlockSpec((1, SB_WIDTH), const),
        pl.BlockSpec((1, SB_WIDTH), const),
    ]


def _inproj_sample(x, nw, w, gseg, qnw, knw):
    b, t, d = x.shape
    widths = [CONV_DIM, GDN_WIDTH, AB_COLS, SB_WIDTH, SB_WIDTH, SB_WIDTH]
    return pl.pallas_call(
        _inproj_sample_kernel,
        name="inproj_sample",
        grid=(b, 1),
        in_specs=_inproj_specs(t, d),
        out_specs=[pl.BlockSpec((1, t, n), lambda i, j: (i, j, 0)) for n in widths],
        out_shape=[jax.ShapeDtypeStruct((b, t, n), F32) for n in widths],
        compiler_params=_cparams(("parallel", "parallel")),
    )(x, nw, w, gseg, qnw, knw)


def _inproj_prompt(x, nw, w, gseg, qnw, knw, cw, pa, pb, stacked, *, layer, depth, tm, t_real):
    b, t, d = x.shape
    row = lambda i, j: (i, j, 0)
    const = lambda i, j: (0, 0)
    hm = lambda i, j: (i, 0, j, 0)
    first_tail = t_real - (CONV_TAPS - 1)
    tail_tile = first_tail // tm
    tail_row = (first_tail % tm) // 8 * 8
    tail_off = first_tail % tm - tail_row
    assert tail_off + CONV_TAPS - 1 <= 8, "conv-state rows must sit in one 8-row group"
    f32_rows = lambda n: (jax.ShapeDtypeStruct((b, t, n), F32), pl.BlockSpec((1, tm, n), row))
    new_kv = (jax.ShapeDtypeStruct((depth, b, SB_WIDTH, t_real), F32),
              pl.BlockSpec((1, 1, SB_WIDTH, tm), lambda i, j: (layer, i, 0, j)))
    outs = [
        f32_rows(GDN_WIDTH), f32_rows(CONV_DIM), f32_rows(AB_COLS),
        (jax.ShapeDtypeStruct((b, 8, t), F32), pl.BlockSpec((1, 8, tm), lambda i, j: (i, 0, j))),
        (jax.ShapeDtypeStruct((b, 8, CONV_DIM), F32), pl.BlockSpec((1, 8, CONV_DIM), lambda i, j: (i, 0, 0))),
        (jax.ShapeDtypeStruct((b, SB_HEADS, t, SB_DH), BF16), pl.BlockSpec((1, SB_HEADS, tm, SB_DH), hm)),
        (jax.ShapeDtypeStruct((b, SB_WIDTH, t), BF16), pl.BlockSpec((1, SB_WIDTH, tm), lambda i, j: (i, 0, j))),
        (jax.ShapeDtypeStruct((b, SB_HEADS, t, SB_DH), BF16), pl.BlockSpec((1, SB_HEADS, tm, SB_DH), hm)),
        new_kv, new_kv,
    ]
    in_specs = _inproj_specs(tm, d) + [pl.BlockSpec((CONV_TAPS, CONV_DIM), const),
                                       pl.BlockSpec((1, AB_COLS), const), pl.BlockSpec((1, AB_COLS), const)]
    args = [x, nw, w, gseg, qnw, knw, cw, pa, pb]
    aliases = {}
    if stacked is not None:
        aliases = {len(args): len(outs) - 2, len(args) + 1: len(outs) - 1}
        in_specs += [pl.BlockSpec(memory_space=pl.ANY)] * 2
        args += list(stacked)
    res = pl.pallas_call(
        functools.partial(_inproj_prompt_kernel, t_real=t_real, tail_tile=tail_tile, tail_row=tail_row),
        name="inproj_prompt",
        grid=(b, t // tm),
        in_specs=in_specs,
        out_specs=[o[1] for o in outs],
        out_shape=[o[0] for o in outs],
        scratch_shapes=[pltpu.VMEM((tm + 8, CONV_DIM), F32), pltpu.VMEM((8, CONV_DIM), F32)],
        input_output_aliases=aliases,
        compiler_params=_cparams(("parallel", "arbitrary")),
    )(*args)
    z, u, gb, gbt, ctail, q, kt, v, knew, vnew = res
    return z, u, gb, gbt, ctail[:, tail_off:tail_off + CONV_TAPS - 1], q, kt, v, (knew, vnew)


def _gdn_inputs(conv_out, ab, pa, pb, valid):
    u = conv_out * _sigmoid(conv_out)
    parts = []
    for hd in range(2 * GDN_HEADS):
        sl = u[:, hd * GDN_D:(hd + 1) * GDN_D]
        nrm = sl * lax.rsqrt(jnp.sum(sl * sl, axis=-1, keepdims=True) + EPS)
        parts.append(nrm * (GDN_D ** -0.5) if hd < GDN_HEADS else nrm)
    parts.append(u[:, 2 * GDN_WIDTH:])
    g = -jnp.exp(pa) * _softplus(ab + pb)
    beta = _sigmoid(ab)
    lane = lax.broadcasted_iota(jnp.int32, ab.shape, 1)
    gb = jnp.where(lane < GDN_HEADS, g, jnp.where(lane < 2 * GDN_HEADS, beta, 0.0))
    if valid is not None:
        gb = jnp.where(valid, gb, 0.0)
    return jnp.concatenate(parts, axis=1), gb


def _conv_sample_kernel(x_ref, st_ref, cw_ref, ab_ref, pa_ref, pb_ref, u_ref, gb_ref, ns_ref):
    c = x_ref.shape[1]
    taps = [st_ref[:, k * c:(k + 1) * c] for k in range(CONV_TAPS - 1)] + [x_ref[...]]
    out = taps[0] * cw_ref[0:1]
    for tap in range(1, CONV_TAPS):
        out = out + taps[tap] * cw_ref[tap:tap + 1]
    u, gb = _gdn_inputs(out, ab_ref[...], pa_ref[...], pb_ref[...], None)
    u_ref[...] = u
    gb_ref[...] = gb
    for k in range(CONV_TAPS - 1):
        ns_ref[:, k * c:(k + 1) * c] = taps[k + 1]


def _conv_sample(conv_in, state, cw, ab, pa, pb):
    n, c = conv_in.shape
    return pl.pallas_call(
        _conv_sample_kernel,
        name="conv_sample",
        out_shape=[
            jax.ShapeDtypeStruct((n, c), F32),
            jax.ShapeDtypeStruct((n, AB_COLS), F32),
            jax.ShapeDtypeStruct((n, (CONV_TAPS - 1) * c), F32),
        ],
        compiler_params=pltpu.CompilerParams(vmem_limit_bytes=VMEM_LIMIT),
    )(conv_in, state, cw, ab, pa, pb)


def _gated_norm(o, z, nw):
    return o * lax.rsqrt(jnp.mean(o * o, axis=-1, keepdims=True) + EPS) * nw * (z * _sigmoid(z))


def _gdn_chunk_kernel(u_ref, gb_ref, gbt_ref, z_ref, nw_ref, ltri_ref, ltrit_ref, o_ref, sout_ref, s_ref):
    step = pl.program_id(1)
    n = GDN_TILE
    c = GDN_CHUNK

    @pl.when(step == 0)
    def _():
        s_ref[...] = jnp.zeros_like(s_ref)

    ltri = ltri_ref[...]
    ltrit = ltrit_ref[...]
    batch = range(u_ref.shape[0])
    gb = [gb_ref[bi] for bi in batch]
    gparts = [_split3(gb[bi]) for bi in batch]
    tparts = [_split3(gbt_ref[bi]) for bi in batch]
    cum_cols = [_dot(ltri, p[0]) + (_dot(ltri, p[1]) + _dot(ltri, p[2])) for p in gparts]
    cum_rows = [_dot(p[0], ltrit) + (_dot(p[1], ltrit) + _dot(p[2], ltrit)) for p in tparts]

    row = lax.broadcasted_iota(jnp.int32, (n, n), 0)
    col = lax.broadcasted_iota(jnp.int32, (n, n), 1)
    same = (row >= c) == (col >= c)
    causal = same & (row >= col)
    strict = same & (row > col)
    eye = (row == col).astype(F32)
    second = lax.broadcasted_iota(jnp.int32, (n, 1), 0) >= c
    zeros_c = jnp.zeros((c, GDN_D), F32)

    chains = [(bi, hd) for bi in batch for hd in range(GDN_HEADS)]
    ids = range(len(chains))
    lanes = lambda hd, base: slice(base + hd * GDN_D, base + (hd + 1) * GDN_D)
    q = [u_ref[bi, :, lanes(hd, 0)] for bi, hd in chains]
    k = [u_ref[bi, :, lanes(hd, GDN_WIDTH)] for bi, hd in chains]
    v = [u_ref[bi, :, lanes(hd, 2 * GDN_WIDTH)] for bi, hd in chains]
    cc = [cum_cols[bi][:, hd:hd + 1] for bi, hd in chains]
    cr = [cum_rows[bi][hd:hd + 1, :] for bi, hd in chains]
    beta = [gb[bi][:, GDN_HEADS + hd:GDN_HEADS + hd + 1] for bi, hd in chains]
    gam = [jnp.where(causal, jnp.exp(jnp.where(causal, cc[x] - cr[x], 0.0)), 0.0) for x in ids]
    kb = [k[x] * beta[x] for x in ids]
    k16 = [k[x].astype(BF16) for x in ids]
    a_mat = [jnp.where(strict, _dot_nt(kb[x].astype(BF16), k16[x]) * gam[x], 0.0) for x in ids]
    qk = [(_dot_nt(q[x].astype(BF16), k16[x]) * gam[x]).astype(BF16) for x in ids]
    inv = [eye - a_mat[x] for x in ids]
    pw = [_dot3(a_mat[x], a_mat[x]) for x in ids]
    for _ in range(4):
        both = [_dot3(jnp.concatenate([inv[x], pw[x]], axis=0), pw[x]) for x in ids]
        inv = [inv[x] + both[x][:n] for x in ids]
        pw = [both[x][n:] for x in ids]
    inv = [inv[x] + _dot3(inv[x], pw[x]) for x in ids]
    ecum = [jnp.exp(cc[x]) for x in ids]
    uw = [_dot3(inv[x], jnp.concatenate([v[x] * beta[x], kb[x] * ecum[x]], axis=1)) for x in ids]
    uu = [uw[x][:, :GDN_D] for x in ids]
    ww = [uw[x][:, GDN_D:].astype(BF16) for x in ids]
    qg = [(q[x] * ecum[x]).astype(BF16) for x in ids]
    g_last0 = [cc[x][c - 1:c] for x in ids]
    g_last1 = [cc[x][n - 1:n] for x in ids]
    kdt = [(k[x] * jnp.exp(jnp.where(second, g_last1[x], g_last0[x]) - cc[x])).T.astype(BF16) for x in ids]

    s0 = [s_ref[bi, hd] for bi, hd in chains]
    s0b = [s0[x].astype(BF16) for x in ids]
    vn0 = [uu[x][:c] - _dot(ww[x][:c], s0b[x]) for x in ids]
    oi0 = [_dot(qg[x][:c], s0b[x]) for x in ids]
    s1 = [s0[x] * jnp.exp(g_last0[x]) + _dot(kdt[x], jnp.concatenate([vn0[x], zeros_c], axis=0).astype(BF16))
          for x in ids]
    s1b = [s1[x].astype(BF16) for x in ids]
    vn1 = [uu[x][c:] - _dot(ww[x][c:], s1b[x]) for x in ids]
    oi1 = [_dot(qg[x][c:], s1b[x]) for x in ids]
    s2 = [s1[x] * jnp.exp(g_last1[x]) + _dot(kdt[x], jnp.concatenate([zeros_c, vn1[x]], axis=0).astype(BF16))
          for x in ids]
    for x, (bi, hd) in enumerate(chains):
        s_ref[bi, hd] = s2[x]
        vn = jnp.concatenate([vn0[x], vn1[x]], axis=0).astype(BF16)
        o = jnp.concatenate([oi0[x], oi1[x]], axis=0) + _dot(qk[x], vn)
        o_ref[bi, :, lanes(hd, 0)] = _gated_norm(o, z_ref[bi, :, lanes(hd, 0)], nw_ref[...]).astype(BF16)

    @pl.when(step == pl.num_programs(1) - 1)
    def _():
        sout_ref[...] = s_ref[...]


def _gdn_prompt(u, gb, gbt, z, nw, ltri, ltrit):
    b, t, _ = u.shape
    n = GDN_TILE
    nb = math.gcd(b, GDN_BATCH_TILE)
    row = lambda i, j: (i, j, 0)
    const = lambda i, j: (0, 0)
    return pl.pallas_call(
        _gdn_chunk_kernel,
        name="gdn_prompt",
        grid=(b // nb, t // n),
        in_specs=[
            pl.BlockSpec((nb, n, CONV_DIM), row),
            pl.BlockSpec((nb, n, AB_COLS), row),
            pl.BlockSpec((nb, 8, n), lambda i, j: (i, 0, j)),
            pl.BlockSpec((nb, n, GDN_WIDTH), row),
            pl.BlockSpec((1, GDN_D), const),
            pl.BlockSpec((n, n), const),
            pl.BlockSpec((n, n), const),
        ],
        out_specs=[
            pl.BlockSpec((nb, n, GDN_WIDTH), row),
            pl.BlockSpec((nb, GDN_HEADS, GDN_D, GDN_D), lambda i, j: (i, 0, 0, 0)),
        ],
        out_shape=[
            jax.ShapeDtypeStruct((b, t, GDN_WIDTH), BF16),
            jax.ShapeDtypeStruct((b, GDN_HEADS, GDN_D, GDN_D), F32),
        ],
        scratch_shapes=[pltpu.VMEM((nb, GDN_HEADS, GDN_D, GDN_D), F32)],
        compiler_params=_cparams(("parallel", "arbitrary")),
    )(u, gb, gbt, z, nw, ltri, ltrit)


def _gdn_step_kernel(s_ref, u_ref, gb_ref, z_ref, nw_ref, *rest):
    sout_ref, o_ref = rest[-2:]
    batch = range(s_ref.shape[0])
    lanes = lambda hd, base: slice(base + hd * GDN_D, base + (hd + 1) * GDN_D)
    cols = [jnp.concatenate([u_ref[bi, :, lanes(hd, base)] for base in (GDN_WIDTH, 0) for hd in range(GDN_HEADS)],
                            axis=0).T for bi in batch]
    units = [(bi, hd) for bi in batch for hd in range(GDN_HEADS)]
    ids = range(len(units))
    kc = [cols[bi][:, hd:hd + 1] for bi, hd in units]
    qc = [cols[bi][:, GDN_HEADS + hd:GDN_HEADS + hd + 1] for bi, hd in units]
    s = [s_ref[bi, hd] * jnp.exp(gb_ref[bi, :, hd:hd + 1]) for bi, hd in units]
    ks = [jnp.sum(kc[x] * s[x], axis=0, keepdims=True) for x in ids]
    upd = [(u_ref[bi, :, lanes(hd, 2 * GDN_WIDTH)] - ks[x]) * gb_ref[bi, :, GDN_HEADS + hd:GDN_HEADS + hd + 1]
           for x, (bi, hd) in enumerate(units)]
    s = [s[x] + kc[x] * upd[x] for x in ids]
    o = [jnp.sum(qc[x] * s[x], axis=0, keepdims=True) for x in ids]
    for x, (bi, hd) in enumerate(units):
        sout_ref[0, bi, hd] = s[x]
        o_ref[bi, :, lanes(hd, 0)] = _gated_norm(o[x], z_ref[bi, :, lanes(hd, 0)], nw_ref[...]).astype(BF16)


def _gdn_sample(state, u, gb, z, nw, stacked, *, layer, depth):
    n = state.shape[0]
    bt = GDN_SAMPLE_TILE
    row = lambda i: (i, 0, 0)
    st = lambda i: (i, 0, 0, 0)
    in_specs = [
        pl.BlockSpec((bt, GDN_HEADS, GDN_D, GDN_D), st),
        pl.BlockSpec((bt, 1, CONV_DIM), row),
        pl.BlockSpec((bt, 1, AB_COLS), row),
        pl.BlockSpec((bt, 1, GDN_WIDTH), row),
        pl.BlockSpec((1, GDN_D), lambda i: (0, 0)),
    ]
    args = [state, u.reshape(n, 1, CONV_DIM), gb.reshape(n, 1, AB_COLS), z.reshape(n, 1, GDN_WIDTH), nw]
    aliases = {}
    if stacked is not None:
        aliases = {len(args): 0}
        in_specs.append(pl.BlockSpec(memory_space=pl.ANY))
        args.append(stacked)
    return pl.pallas_call(
        _gdn_step_kernel,
        name="gdn_sample",
        grid=(n // bt,),
        in_specs=in_specs,
        out_specs=[
            pl.BlockSpec((1, bt, GDN_HEADS, GDN_D, GDN_D), lambda i: (layer, i, 0, 0, 0)),
            pl.BlockSpec((bt, 1, GDN_WIDTH), row),
        ],
        out_shape=[
            jax.ShapeDtypeStruct((depth,) + state.shape, F32),
            jax.ShapeDtypeStruct((n, 1, GDN_WIDTH), BF16),
        ],
        input_output_aliases=aliases,
        compiler_params=_cparams(("parallel",)),
    )(*args)


def _neg_abs(x):
    bits = lax.bitcast_convert_type(x, jnp.uint32) | jnp.uint32(0x80000000)
    return lax.bitcast_convert_type(bits, F32)


def _sb_blocks(zns, later, carries, mask):
    nk = zns[0][0].shape[1]
    stays = [[jnp.minimum(zn, 0.0) - jnp.log2(1.0 + jnp.exp2(_neg_abs(zn))) for zn in ch] for ch in zns]
    if mask is not None:
        stays = [[jnp.where(mask, s, 0.0) for s in ch] for ch in stays]
    scans = [[_dot(s.astype(BF16), later) for s in ch] for ch in stays]
    ws, out_carries = [], []
    for c, carry in enumerate(carries):
        ws.append([])
        for zn, stay, scan in zip(zns[c], stays[c], scans[c]):
            w = jnp.exp2((stay - zn) + (scan[:, :nk] + jnp.concatenate([carry] * (nk // 128), axis=1)))
            if mask is not None:
                w = jnp.where(mask, w, 0.0)
            ws[c].append(w.astype(BF16))
            carry = carry + scan[:, nk:]
        out_carries.append(carry)
    return ws, out_carries


def _sb_prompt_kernel(bias_ref, q_ref, kt_ref, v_ref, later_ref, o_ref, zn_ref, w_ref, acc_ref, car_ref, *,
                      last_rows):
    hp = pl.program_id(1)
    i = pl.program_id(2)
    nq = q_ref.shape[2]
    heads = range(SB_GROUP)

    def span(blk):
        return pl.ds(pl.multiple_of(blk * nq, nq), nq)

    def run(nr):
        rows = slice(0, nr)
        diag_mask = lax.broadcasted_iota(jnp.int32, (nr, nq), 1) < lax.broadcasted_iota(jnp.int32, (nr, nq), 0)

        def logits(blk, slot):
            for hh in heads:
                zn_ref[slot, hh, rows] = _dot(q_ref[0, hh, rows], kt_ref[0, hh, :, span(blk)]) + bias_ref[SB_GROUP * hp + hh]

        def weights(slot, mask):
            ws, carries = _sb_blocks([[zn_ref[slot, hh, rows]] for hh in heads], later_ref[...],
                                     [car_ref[hh, rows] for hh in heads], mask)
            for hh in heads:
                w_ref[hh, rows] = ws[hh][0]
                car_ref[hh, rows] = carries[hh]

        def attend(blk):
            for hh in heads:
                acc_ref[hh, rows] += _dot(w_ref[hh, rows], v_ref[0, hh, span(blk), :])

        def step(s, slot):
            attend(i - s + 1)
            logits(jnp.maximum(i - s - 1, 0), 1 - slot)
            weights(slot, None)

        acc_ref[...] = jnp.zeros_like(acc_ref)
        car_ref[...] = jnp.zeros_like(car_ref)
        logits(i, 0)
        logits(jnp.maximum(i - 1, 0), 1)
        weights(0, diag_mask)

        def body(p, _):
            step(2 * p + 1, 1)
            step(2 * p + 2, 0)
            return 0

        lax.fori_loop(0, i // 2, body, 0)

        @pl.when(i % 2 == 1)
        def _():
            step(i, 1)

        attend(0)
        o_ref[0] = jnp.concatenate([acc_ref[hh] for hh in heads], axis=1).astype(BF16)

    if last_rows == nq:
        run(nq)
    else:
        last = pl.num_programs(2) - 1
        pl.when(i < last)(lambda: run(nq))
        pl.when(i == last)(lambda: run(last_rows))


def _sb_prompt(q, kt, v, bias, later, *, tq, t_real):
    b, nh, t, dh = q.shape
    kt = kt.reshape(b, nh, dh, t)
    last_rows = min(tq, -(-(t_real - (t // tq - 1) * tq) // BF16_ROWS) * BF16_ROWS)
    return pl.pallas_call(
        functools.partial(_sb_prompt_kernel, last_rows=last_rows),
        name="sb_prompt",
        grid_spec=pltpu.PrefetchScalarGridSpec(
            num_scalar_prefetch=0,
            grid=(b, nh // SB_GROUP, t // tq),
            in_specs=[
                pl.BlockSpec(memory_space=pltpu.SMEM),
                pl.BlockSpec((1, SB_GROUP, tq, dh), lambda i, h, j: (i, h, j, 0)),
                pl.BlockSpec((1, SB_GROUP, dh, t), lambda i, h, j: (i, h, 0, 0)),
                pl.BlockSpec((1, SB_GROUP, t, dh), lambda i, h, j: (i, h, 0, 0)),
                pl.BlockSpec((tq, tq + 128), lambda i, h, j: (0, 0)),
            ],
            out_specs=pl.BlockSpec((1, tq, SB_GROUP * dh), lambda i, h, j: (i, j, h)),
            scratch_shapes=[pltpu.VMEM((2, SB_GROUP, tq, tq), F32), pltpu.VMEM((SB_GROUP, tq, tq), BF16),
                            pltpu.VMEM((SB_GROUP, tq, dh), F32), pltpu.VMEM((SB_GROUP, tq, 128), F32)],
        ),
        out_shape=jax.ShapeDtypeStruct((b, t, nh * dh), BF16),
        compiler_params=_cparams(("parallel", "parallel", "arbitrary")),
    )(bias, q, kt, v, later)


def _sb_decode_kernel(pt_ref, q_ref, bias_ref, later_ref, *refs, n_pages):
    k_refs = refs[:n_pages]
    v_refs = refs[n_pages:2 * n_pages]
    o_ref = refs[2 * n_pages]
    head_of_lane = lax.broadcasted_iota(jnp.int32, (SB_HEADS, SB_WIDTH), 1) // SB_DH
    own = head_of_lane == lax.broadcasted_iota(jnp.int32, (SB_HEADS, SB_WIDTH), 0)
    qbd = jnp.where(own, q_ref[0], 0.0).astype(BF16)
    order = range(n_pages - 1, -1, -1)
    zs = [_dot(qbd, k_refs[p][0].astype(BF16)) + bias_ref[...] for p in order]
    ws, _ = _sb_blocks([zs], later_ref[...], [jnp.zeros((SB_HEADS, 128), F32)], None)
    acc = _dot_nt(ws[0][0], v_refs[order[0]][0].astype(BF16))
    for w, p in zip(ws[0][1:], order[1:]):
        acc = acc + _dot_nt(w, v_refs[p][0].astype(BF16))
    o_ref[0] = jnp.sum(jnp.where(own, acc, 0.0), axis=0, keepdims=True).astype(BF16)


def _sb_decode(q, cache_k, cache_v, page_table, bias_b, later, *, layer):
    n, n_pages = page_table.shape
    depth, n_pool, page, nh, dh = cache_k.shape
    ck = jnp.transpose(cache_k, (0, 1, 3, 4, 2)).reshape(depth * n_pool, nh * dh, page)
    cv = jnp.transpose(cache_v, (0, 1, 3, 4, 2)).reshape(depth * n_pool, nh * dh, page)

    def page_spec(p):
        return pl.BlockSpec((1, nh * dh, page), lambda i, pt: (layer * n_pool + pt[i * n_pages + p], 0, 0))

    row = lambda i, pt: (i, 0, 0)
    return pl.pallas_call(
        functools.partial(_sb_decode_kernel, n_pages=n_pages),
        name="sb_decode",
        grid_spec=pltpu.PrefetchScalarGridSpec(
            num_scalar_prefetch=1,
            grid=(n,),
            in_specs=[
                pl.BlockSpec((1, 1, nh * dh), row),
                pl.BlockSpec((nh, 128), lambda i, pt: (0, 0)),
                pl.BlockSpec((page, page + 128), lambda i, pt: (0, 0)),
            ] + [page_spec(p) for p in range(n_pages)] * 2,
            out_specs=pl.BlockSpec((1, 1, nh * dh), row),
        ),
        out_shape=jax.ShapeDtypeStruct((n, 1, nh * dh), BF16),
        compiler_params=_cparams(("parallel",)),
    )(page_table.reshape(-1), q.reshape(n, 1, nh * dh), bias_b, later, *([ck] * n_pages), *([cv] * n_pages))


def _mlp_kernel(x_ref, oa_ref, ob_ref, wo_ref, nw_ref, wup_ref, wdn_ref, y_ref, x1_ref, xn_ref, acc_ref):
    j = pl.program_id(1)

    @pl.when(j == 0)
    def _():
        x1 = x_ref[...] + (_dot(oa_ref[...], wo_ref[:GDN_WIDTH]) + _dot(ob_ref[...], wo_ref[GDN_WIDTH:]))
        x1_ref[...] = x1
        ms = jnp.mean(x1 * x1, axis=-1, keepdims=True)
        xn_ref[...] = (x1 * lax.rsqrt(ms + EPS) * nw_ref[...]).astype(BF16)
        acc_ref[...] = jnp.zeros_like(acc_ref)

    h = jnp.maximum(_dot(xn_ref[...], wup_ref[...]), 0.0)
    acc_ref[...] += _dot((h * h).astype(BF16), wdn_ref[...])

    @pl.when(j == pl.num_programs(1) - 1)
    def _():
        y_ref[...] = x1_ref[...] + acc_ref[...]


def _mix_mlp(x, oa, ob, wo, nw, wup, wdn, *, tm, tf):
    m, d = x.shape
    ff = wup.shape[1]
    row = lambda i, j: (i, 0)
    const = lambda i, j: (0, 0)
    return pl.pallas_call(
        _mlp_kernel,
        name="mix_mlp",
        grid=(m // tm, ff // tf),
        in_specs=[
            pl.BlockSpec((tm, d), row),
            pl.BlockSpec((tm, GDN_WIDTH), row),
            pl.BlockSpec((tm, SB_WIDTH), row),
            pl.BlockSpec((d, d), const),
            pl.BlockSpec((1, d), const),
            pl.BlockSpec((d, tf), lambda i, j: (0, j)),
            pl.BlockSpec((tf, d), lambda i, j: (j, 0)),
        ],
        out_specs=pl.BlockSpec((tm, d), row),
        out_shape=jax.ShapeDtypeStruct((m, d), F32),
        scratch_shapes=[pltpu.VMEM((tm, d), F32), pltpu.VMEM((tm, d), BF16), pltpu.VMEM((tm, d), F32)],
        compiler_params=_cparams(("parallel", "arbitrary")),
    )(x, oa, ob, wo, nw, wup, wdn)


def _tri_consts(n_chunk_tile, chunk, tq, page):
    r = jnp.arange(n_chunk_tile)
    same = (r[:, None] // chunk) == (r[None, :] // chunk)
    ltri = (same & (r[None, :] <= r[:, None])).astype(BF16)

    def later(nk):
        k = jnp.arange(nk)
        after = (k[:, None] > k[None, :]).astype(BF16)
        return jnp.concatenate([after, jnp.ones((nk, 128), BF16)], axis=1)

    return ltri, ltri.T, later(tq), later(page)


def kernel(x_prompt, x_sample, cache_k, cache_v, page_table, state_gdn, state_conv, meta_tokens, norm1_w, w_in,
           conv_w, a_log, dt_bias, gdn_norm_w, q_norm_w, k_norm_w, sb_bias, w_o, norm2_w, w_up, w_down):
    depth, d_model = norm1_w.shape
    bp, seq, _ = x_prompt.shape
    bs = x_sample.shape[0]
    page = cache_k.shape[2]
    t_real = N_META + seq
    tp = -(-t_real // ROW_TILE) * ROW_TILE

    c0 = CONV_DIM
    c1 = c0 + GDN_WIDTH
    c2 = c1 + GDN_HEADS
    c3 = c2 + GDN_HEADS
    w_in_r = jnp.concatenate(
        [w_in[..., :c1], w_in[..., c3:], w_in[..., c1:c3],
         jnp.zeros((depth, d_model, AB_COLS - 2 * GDN_HEADS), w_in.dtype)], axis=-1).astype(BF16)
    w_o16, w_up16, w_dn16 = w_o.astype(BF16), w_up.astype(BF16), w_down.astype(BF16)
    pad_lanes = lambda a: jnp.pad(a, ((0, 0), (0, AB_COLS - a.shape[1])))[:, None, :]
    pa, pb = pad_lanes(a_log), pad_lanes(dt_bias)
    qnw = jnp.tile(q_norm_w, (1, SB_HEADS))[:, None, :]
    knw = jnp.tile(k_norm_w, (1, SB_HEADS))[:, None, :]
    lane_head = jnp.arange(SB_WIDTH) // SB_DH
    gseg = (lane_head[:, None] == lane_head[None, :]).astype(BF16)
    ltri, ltrit, later_q, later_p = _tri_consts(GDN_TILE, GDN_CHUNK, ROW_TILE, page)
    sb_bias2 = -sb_bias * LOG2E
    bias_b = jnp.broadcast_to(sb_bias2[:, :, None], (depth, SB_HEADS, 128))

    meta = jnp.broadcast_to(meta_tokens[None], (bp, N_META, d_model))
    xp = jnp.concatenate([meta, x_prompt, jnp.zeros((bp, tp - t_real, d_model), x_prompt.dtype)], axis=1)
    xs = x_sample.reshape(1, bs, d_model)
    st_conv = state_conv.reshape(depth, bs, (CONV_TAPS - 1) * CONV_DIM)

    sp_l, cp_l = [], []
    kv_prompt = None
    ks_l, vs_l, cs_l = [], [], []
    ss_all = None
    for l in range(depth):
        n1, n2 = norm1_w[l][None], norm2_w[l][None]
        gnw = gdn_norm_w[l][None]
        z, u, gb, gbt, ctail, q, kt, v, kv_prompt = _inproj_prompt(
            xp, n1, w_in_r[l], gseg, qnw[l], knw[l], conv_w[l], pa[l], pb[l], kv_prompt,
            layer=l, depth=depth, tm=ROW_TILE, t_real=t_real)
        o_a, s_fin = _gdn_prompt(u, gb, gbt, z, gnw, ltri, ltrit)
        o_b = _sb_prompt(q, kt, v, sb_bias2[l], later_q, tq=ROW_TILE, t_real=t_real)
        xp = _mix_mlp(xp.reshape(bp * tp, d_model), o_a.reshape(bp * tp, GDN_WIDTH), o_b.reshape(bp * tp, SB_WIDTH),
                      w_o16[l], n2, w_up16[l], w_dn16[l], tm=1024, tf=1024).reshape(bp, tp, d_model)
        sp_l.append(s_fin)
        cp_l.append(ctail)

        conv_in, z, ab, q, knew, vnew = _inproj_sample(xs, n1, w_in_r[l], gseg, qnw[l], knw[l])
        u, gb, ns = _conv_sample(conv_in[0], st_conv[l], conv_w[l], ab[0], pa[l], pb[l])
        ss_all, o_a = _gdn_sample(state_gdn[l], u, gb, z[0], gnw, ss_all, layer=l, depth=depth)
        o_b = _sb_decode(q[0], cache_k, cache_v, page_table, bias_b[l], later_p, layer=l)
        xs = _mix_mlp(xs[0], o_a.reshape(bs, GDN_WIDTH), o_b.reshape(bs, SB_WIDTH), w_o16[l], n2, w_up16[l],
                      w_dn16[l], tm=bs, tf=1024)[None]
        ks_l.append(knew.reshape(bs, 1, SB_HEADS, SB_DH))
        vs_l.append(vnew.reshape(bs, 1, SB_HEADS, SB_DH))
        cs_l.append(ns.reshape(bs, CONV_TAPS - 1, CONV_DIM))

    heads_last = lambda a: a.reshape(depth, bp, SB_HEADS, SB_DH, t_real).transpose(0, 1, 4, 2, 3)
    y_prompt = xp[:, N_META:t_real]
    return (y_prompt, xs.reshape(bs, 1, d_model),
            heads_last(kv_prompt[0]), heads_last(kv_prompt[1]), jnp.stack(sp_l), jnp.stack(cp_l),
            jnp.stack(ks_l), jnp.stack(vs_l), ss_all, jnp.stack(cs_l))
```
